```python
import jax, jax.numpy as jnp
from jax import lax
import numpy as np


D_MODEL = 1024
BATCH = 16
SEQ = 4096
DEPTH = 1
DEC_BATCH = 8
DEC_SEQ = 16
PAST_LEN = 2048

CHUNK = 64
LEFT_CHUNKS = 8
BAND = LEFT_CHUNKS * CHUNK
N_HEADS = 16
HEAD_DIM = 64
ATTN_WIDTH = N_HEADS * HEAD_DIM
ATTN_SCALE = HEAD_DIM ** -0.5
MAX_REL = 128
LRU_WIDTH = 1024
LRU_BLOCKS = 16
LRU_BLOCK = LRU_WIDTH // LRU_BLOCKS
CONV_W = 4
LRU_C = 8.0
PEER_HEADS = 8
N_KEYS = 128
N_EXPERTS = N_KEYS * N_KEYS
D_KEY = 256
HALF_KEY = D_KEY // 2
TOPK_HALF = 16
TOPK = 16
TOKEN_BLOCK = 128
EPS = 1e-6
NEG_INF = -1e30
IN_COLS = 3 * ATTN_WIDTH + 2 * LRU_WIDTH + 2 * D_MODEL
SPLITS = (ATTN_WIDTH, 2 * ATTN_WIDTH, 3 * ATTN_WIDTH, 3 * ATTN_WIDTH + LRU_WIDTH,
          3 * ATTN_WIDTH + 2 * LRU_WIDTH, 3 * ATTN_WIDTH + 2 * LRU_WIDTH + D_MODEL)

kernel_name = 'hybrid_stream_band_rglru_peer'


def rmsnorm(x, g):
    xf = x.astype(jnp.float32)
    y = xf * lax.rsqrt(jnp.mean(xf * xf, axis=-1, keepdims=True) + EPS)
    return (y * g.astype(jnp.float32)).astype(x.dtype)


def rel_bias(rel_table, q_pos, k_pos):
    idx = jnp.clip(q_pos[:, None] - k_pos[None, :], -MAX_REL, MAX_REL) + MAX_REL
    return rel_table[:, idx].astype(jnp.float32)


def band_attention_prompt(q, k, v, rel_table):
    bsz, seq = q.shape[0], q.shape[1]
    n_chunks = seq // CHUNK
    span = BAND + CHUNK
    pad = ((0, 0), (BAND, 0), (0, 0), (0, 0))
    k_pad = jnp.pad(k, pad)
    v_pad = jnp.pad(v, pad)
    offs = jnp.arange(span)
    bias = rel_bias(rel_table, jnp.arange(CHUNK) + BAND, offs)
    q_chunks = jnp.moveaxis(q.reshape(bsz, n_chunks, CHUNK, N_HEADS, HEAD_DIM), 1, 0)

    def one_chunk(args):
        c, q_c = args
        start = c * CHUNK
        k_b = lax.dynamic_slice_in_dim(k_pad, start, span, axis=1)
        v_b = lax.dynamic_slice_in_dim(v_pad, start, span, axis=1)
        valid = (start - BAND + offs) >= 0
        s = jnp.einsum('bqhd,bkhd->bhqk', q_c, k_b).astype(jnp.float32) * ATTN_SCALE + bias
        s = jnp.where(valid[None, None, None, :], s, NEG_INF)
        p = jax.nn.softmax(s, axis=-1).astype(v_b.dtype)
        return jnp.einsum('bhqk,bkhd->bqhd', p, v_b)

    o = lax.map(one_chunk, (jnp.arange(n_chunks), q_chunks))
    return jnp.moveaxis(o, 0, 1).reshape(bsz, seq, ATTN_WIDTH)


def band_attention_sample(q, k_new, v_new, cache_k, cache_v, rel_table):
    bsz, t_new = q.shape[0], q.shape[1]
    rows = cache_k.shape[1]
    k_all = jnp.concatenate([cache_k, k_new], axis=1)
    v_all = jnp.concatenate([cache_v, v_new], axis=1)
    q_pos = PAST_LEN + jnp.arange(t_new)
    k_pos = PAST_LEN - rows + jnp.arange(rows + t_new)
    bias = rel_bias(rel_table, q_pos, k_pos)
    s = jnp.einsum('bqhd,bkhd->bhqk', q, k_all).astype(jnp.float32) * ATTN_SCALE + bias
    p = jax.nn.softmax(s, axis=-1).astype(v_all.dtype)
    return jnp.einsum('bhqk,bkhd->bqhd', p, v_all).reshape(bsz, t_new, ATTN_WIDTH)


def causal_conv(x, conv_state, conv_w, conv_b):
    t = x.shape[1]
    xp = jnp.concatenate([conv_state, x], axis=1)
    y = conv_b
    for i in range(CONV_W):
        y = y + xp[:, i:i + t] * conv_w[i]
    return y, xp[:, xp.shape[1] - (CONV_W - 1):]


def _linear_combine(c1, c2):
    a1, b1 = c1
    a2, b2 = c2
    return a1 * a2, a2 * b1 + b2


def rg_lru(x, h0, w_r, b_r, w_i, b_i, lam):
    bsz, t, width = x.shape
    xf = x.astype(jnp.float32)
    xb = xf.reshape(bsz, t, LRU_BLOCKS, LRU_BLOCK)
    r = jax.nn.sigmoid(jnp.einsum('btgi,gij->btgj', xb, w_r.astype(jnp.float32)).reshape(bsz, t, width)
                       + b_r.astype(jnp.float32))
    gi = jax.nn.sigmoid(jnp.einsum('btgi,gij->btgj', xb, w_i.astype(jnp.float32)).reshape(bsz, t, width)
                        + b_i.astype(jnp.float32))
    log_a = -LRU_C * r * jax.nn.softplus(-lam.astype(jnp.float32))
    a = jnp.exp(log_a)
    b = jnp.sqrt(-jnp.expm1(2.0 * log_a)) * (gi * xf)
    b = b.at[:, 0].add(a[:, 0] * h0.astype(jnp.float32))
    _, h = lax.associative_scan(_linear_combine, (a, b), axis=1)
    return h.astype(x.dtype), h[:, -1].astype(x.dtype)


def mixer_sublayer(x, conv_state, h0, attend, norm_g, w_in, conv_w, conv_b, w_r, b_r, w_i, b_i, lam,
                   w_ba, w_bl, w_o):
    bsz, t = x.shape[0], x.shape[1]
    xn = rmsnorm(x, norm_g)
    proj = jnp.einsum('btd,dc->btc', xn, w_in)
    q, k, v, x_r, g_branch, gate_a, gate_r = jnp.split(proj, SPLITS, axis=-1)
    q = q.reshape(bsz, t, N_HEADS, HEAD_DIM)
    k = k.reshape(bsz, t, N_HEADS, HEAD_DIM)
    v = v.reshape(bsz, t, N_HEADS, HEAD_DIM)
    o_attn = attend(q, k, v)
    x_conv, conv_new = causal_conv(x_r, conv_state, conv_w, conv_b)
    h, h_last = rg_lru(x_conv, h0, w_r, b_r, w_i, b_i, lam)
    o_lru = h * jax.nn.gelu(g_branch)
    merged = jax.nn.sigmoid(gate_a) * (o_attn @ w_ba) + jax.nn.sigmoid(gate_r) * (o_lru @ w_bl)
    return x + merged @ w_o, k, v, conv_new, h_last


def peer_tokens(xt, wq, keys1, keys2, expert_u, expert_v):
    t = xt.shape[0]
    q = (xt @ wq).reshape(t, PEER_HEADS, 2, HALF_KEY)
    s1 = jnp.einsum('thd,hnd->thn', q[:, :, 0], keys1).astype(jnp.float32)
    s2 = jnp.einsum('thd,hnd->thn', q[:, :, 1], keys2).astype(jnp.float32)
    v1, i1 = lax.top_k(s1, TOPK_HALF)
    v2, i2 = lax.top_k(s2, TOPK_HALF)
    cand_s = (v1[..., :, None] + v2[..., None, :]).reshape(t, PEER_HEADS, TOPK_HALF * TOPK_HALF)
    cand_i = (i1[..., :, None] * N_KEYS + i2[..., None, :]).reshape(t, PEER_HEADS, TOPK_HALF * TOPK_HALF)
    top_s, pos = lax.top_k(cand_s, TOPK)
    e_idx = jnp.take_along_axis(cand_i, pos, axis=-1)
    g = jax.nn.softmax(top_s, axis=-1)
    u_sel = expert_u[e_idx]
    act = jax.nn.gelu(jnp.einsum('thkd,td->thk', u_sel, xt).astype(jnp.float32))
    return jnp.einsum('thk,thkd->td', (g * act).astype(xt.dtype), expert_v[e_idx])


def peer_sublayer(x, norm_g, wq, keys1, keys2, expert_u, expert_v, blocked):
    xt = rmsnorm(x, norm_g).reshape(-1, D_MODEL)
    if blocked:
        out = lax.map(lambda xb: peer_tokens(xb, wq, keys1, keys2, expert_u, expert_v),
                      xt.reshape(-1, TOKEN_BLOCK, D_MODEL))
    else:
        out = peer_tokens(xt, wq, keys1, keys2, expert_u, expert_v)
    return x + out.reshape(x.shape)


def setup_inputs(seed: int = 0) -> dict:
    key = jax.random.key(seed)
    ks = jax.random.split(key, 26)
    kv_rows = min(BAND, PAST_LEN)

    def nrm(k, shape, scale):
        return jax.random.normal(k, shape, jnp.float32) * scale

    u = jax.random.uniform(ks[15], (DEPTH, LRU_WIDTH), jnp.float32, 0.9, 0.999)
    a_base = u ** (1.0 / LRU_C)
    lam = jnp.log(a_base) - jnp.log1p(-a_base)
    return {
        'x_prompt': nrm(ks[0], (BATCH, SEQ, D_MODEL), 1.0),
        'x_sample': nrm(ks[1], (DEC_BATCH, DEC_SEQ, D_MODEL), 1.0),
        'cache_k': nrm(ks[2], (DEPTH, DEC_BATCH, kv_rows, N_HEADS, HEAD_DIM), 1.0),
        'cache_v': nrm(ks[3], (DEPTH, DEC_BATCH, kv_rows, N_HEADS, HEAD_DIM), 1.0),
        'state_conv': nrm(ks[4], (DEPTH, DEC_BATCH, CONV_W - 1, LRU_WIDTH), 1.0),
        'state_lru': nrm(ks[5], (DEPTH, DEC_BATCH, LRU_WIDTH), 0.5),
        'norm_mix': 1.0 + nrm(ks[6], (DEPTH, D_MODEL), 0.02),
        'w_in': nrm(ks[7], (DEPTH, D_MODEL, IN_COLS), D_MODEL ** -0.5),
        'rel_table': nrm(ks[8], (DEPTH, N_HEADS, 2 * MAX_REL + 1), 0.1),
        'conv_w': nrm(ks[9], (DEPTH, CONV_W, LRU_WIDTH), CONV_W ** -0.5),
        'conv_b': nrm(ks[10], (DEPTH, LRU_WIDTH), 0.01),
        'lru_wr': nrm(ks[11], (DEPTH, LRU_BLOCKS, LRU_BLOCK, LRU_BLOCK), LRU_BLOCK ** -0.5),
        'lru_br': nrm(ks[12], (DEPTH, LRU_WIDTH), 0.01),
        'lru_wi': nrm(ks[13], (DEPTH, LRU_BLOCKS, LRU_BLOCK, LRU_BLOCK), LRU_BLOCK ** -0.5),
        'lru_bi': nrm(ks[14], (DEPTH, LRU_WIDTH), 0.01),
        'lru_lambda': lam,
        'w_branch_attn': nrm(ks[16], (DEPTH, ATTN_WIDTH, D_MODEL), ATTN_WIDTH ** -0.5),
        'w_branch_lru': nrm(ks[17], (DEPTH, LRU_WIDTH, D_MODEL), LRU_WIDTH ** -0.5),
        'w_out': nrm(ks[18], (DEPTH, D_MODEL, D_MODEL), D_MODEL ** -0.5),
        'norm_ffn': 1.0 + nrm(ks[19], (DEPTH, D_MODEL), 0.02),
        'peer_wq': nrm(ks[20], (DEPTH, D_MODEL, PEER_HEADS * D_KEY), D_MODEL ** -0.5),
        'peer_keys1': nrm(ks[21], (DEPTH, PEER_HEADS, N_KEYS, HALF_KEY), HALF_KEY ** -0.5),
        'peer_keys2': nrm(ks[22], (DEPTH, PEER_HEADS, N_KEYS, HALF_KEY), HALF_KEY ** -0.5),
        'peer_u': nrm(ks[23], (DEPTH, N_EXPERTS, D_MODEL), D_MODEL ** -0.5),
        'peer_v': nrm(ks[24], (DEPTH, N_EXPERTS, D_MODEL), D_MODEL ** -0.5),
        'norm_final': 1.0 + nrm(ks[25], (D_MODEL,), 0.02),
    }


def reference(x_prompt, x_sample, cache_k, cache_v, state_conv, state_lru, norm_mix, w_in, rel_table,
              conv_w, conv_b, lru_wr, lru_br, lru_wi, lru_bi, lru_lambda, w_branch_attn, w_branch_lru,
              w_out, norm_ffn, peer_wq, peer_keys1, peer_keys2, peer_u, peer_v, norm_final):
    xp = x_prompt
    xs = x_sample
    k_prompt_l, v_prompt_l, conv_prompt_l, lru_prompt_l = [], [], [], []
    k_sample_l, v_sample_l, conv_sample_l, lru_sample_l = [], [], [], []
    for l in range(DEPTH):
        mix_w = (norm_mix[l], w_in[l], conv_w[l], conv_b[l], lru_wr[l], lru_br[l], lru_wi[l], lru_bi[l],
                 lru_lambda[l], w_branch_attn[l], w_branch_lru[l], w_out[l])
        ffn_w = (norm_ffn[l], peer_wq[l], peer_keys1[l], peer_keys2[l], peer_u[l], peer_v[l])
        rt = rel_table[l]
        ck = cache_k[l]
        cv = cache_v[l]
        conv0 = jnp.zeros((xp.shape[0], CONV_W - 1, LRU_WIDTH), xp.dtype)
        h_zero = jnp.zeros((xp.shape[0], LRU_WIDTH), xp.dtype)
        xp, k_p, v_p, c_p, h_p = mixer_sublayer(
            xp, conv0, h_zero, lambda q, k, v: band_attention_prompt(q, k, v, rt), *mix_w)
        xp = peer_sublayer(xp, *ffn_w, True)
        xs, k_s, v_s, c_s, h_s = mixer_sublayer(
            xs, state_conv[l], state_lru[l],
            lambda q, k, v: band_attention_sample(q, k, v, ck, cv, rt), *mix_w)
        xs = peer_sublayer(xs, *ffn_w, False)
        rows_p = min(BAND, k_p.shape[1])
        k_prompt_l.append(k_p[:, k_p.shape[1] - rows_p:])
        v_prompt_l.append(v_p[:, v_p.shape[1] - rows_p:])
        conv_prompt_l.append(c_p)
        lru_prompt_l.append(h_p)
        k_sample_l.append(k_s)
        v_sample_l.append(v_s)
        conv_sample_l.append(c_s)
        lru_sample_l.append(h_s)
    y_prompt = rmsnorm(xp, norm_final)
    y_sample = rmsnorm(xs, norm_final)
    return (y_prompt, y_sample,
            jnp.stack(k_prompt_l), jnp.stack(v_prompt_l), jnp.stack(conv_prompt_l), jnp.stack(lru_prompt_l),
            jnp.stack(k_sample_l), jnp.stack(v_sample_l), jnp.stack(conv_sample_l), jnp.stack(lru_sample_l))
```

```python
import functools

import jax
import jax.numpy as jnp
from jax import lax
from jax.experimental import pallas as pl
from jax.experimental.pallas import tpu as pltpu

CHUNK = 64
LEFT_CHUNKS = 8
BAND = LEFT_CHUNKS * CHUNK
N_HEADS = 16
HEAD_DIM = 64
ATTN_SCALE = HEAD_DIM ** -0.5
MAX_REL = 128
PAST_LEN = 2048
LRU_BLOCKS = 16
CONV_W = 4
LRU_C = 8.0
PEER_HEADS = 8
N_KEYS = 128
TOPK = 16
EPS = 1e-6
NEG_INF = -1e30

LANES = 128
SUBLANES = 8
MXU_DIM = 256
TOKEN_TILE = 128
SLAB = 4
SLAB_STRIDE = 136
VMEM_LIMIT = 56 * 1024 * 1024

_bf16 = jnp.bfloat16
_f32 = jnp.float32


def _row_tile(n_rows):
    for t in (512, 384, 256, 128):
        if n_rows % t == 0:
            return t
    raise ValueError(f"token count {n_rows} must be a multiple of 128")


def _rms(x, g):
    return x * lax.rsqrt(jnp.mean(x * x, axis=-1, keepdims=True) + EPS) * g


def _inproj_kernel(x_ref, g_ref, w_ref, o_ref, xn_ref):
    @pl.when(pl.program_id(1) == 0)
    def _():
        xn_ref[...] = _rms(x_ref[...], g_ref[...]).astype(_bf16)

    o_ref[...] = jnp.dot(xn_ref[...], w_ref[...], preferred_element_type=_f32)


def _inproj(x, g, w):
    n, d = x.shape
    cols = w.shape[1]
    tm = _row_tile(n)
    return pl.pallas_call(
        _inproj_kernel,
        grid=(n // tm, cols // d),
        in_specs=[pl.BlockSpec((tm, d), lambda i, j: (i, 0)),
                  pl.BlockSpec((1, d), lambda i, j: (0, 0)),
                  pl.BlockSpec((d, d), lambda i, j: (0, j))],
        out_specs=pl.BlockSpec((tm, d), lambda i, j: (i, j)),
        out_shape=jax.ShapeDtypeStruct((n, cols), _f32),
        scratch_shapes=[pltpu.VMEM((tm, d), _bf16)],
        compiler_params=pltpu.CompilerParams(
            dimension_semantics=("parallel", "arbitrary"), vmem_limit_bytes=VMEM_LIMIT),
        name="inproj",
    )(x, g, w)


def _attn_kernel(q_ref, kp_ref, kc_ref, vp_ref, vc_ref, bias_ref, o_ref, kcat, vcat,
                 *, n_chunks, cq, mask_first):
    nq = n_chunks * cq
    span = BAND + cq
    kcat[0:BAND, :] = kp_ref[...].astype(_bf16)
    kcat[BAND:BAND + nq, :] = kc_ref[...].astype(_bf16)
    vcat[0:BAND, :] = vp_ref[...].astype(_bf16)
    vcat[BAND:BAND + nq, :] = vc_ref[...].astype(_bf16)
    first = pl.program_id(1) == 0

    def chunk(cl, carry):
        r0 = pl.multiple_of(cl * cq, cq)
        q = (q_ref[pl.ds(r0, cq), :] * ATTN_SCALE).astype(_bf16)
        kw = kcat[pl.ds(r0, span), :]
        vw = vcat[pl.ds(r0, span), :]
        if mask_first:
            col = lax.broadcasted_iota(jnp.int32, (cq, span), 1)
            valid = jnp.logical_or(jnp.logical_not(first), col + r0 >= BAND)
        outs = []
        for h in range(N_HEADS):
            sl = slice(h * HEAD_DIM, (h + 1) * HEAD_DIM)
            s = lax.dot_general(q[:, sl], kw[:, sl], (((1,), (1,)), ((), ())),
                                preferred_element_type=_f32)
            s = s + bias_ref[h]
            if mask_first:
                s = jnp.where(valid, s, NEG_INF)
            m = jnp.max(s, axis=-1, keepdims=True)
            p = jnp.exp(s - m)
            l = jnp.sum(p, axis=-1, keepdims=True)
            o = jnp.dot(p.astype(_bf16), vw[:, sl], preferred_element_type=_f32)
            outs.append(o / l)
        o_ref[pl.ds(r0, cq), :] = jnp.concatenate(outs, axis=-1).astype(o_ref.dtype)
        return carry

    lax.fori_loop(0, n_chunks, chunk, 0)


def _attention(q_src, kprev_src, vprev_src, bias, *, out_rows, n_chunks, cq, mask_first, grid,
               q_map, k_map, v_map, kprev_spec, vprev_spec):
    nq = n_chunks * cq
    d = N_HEADS * HEAD_DIM
    span = BAND + cq
    kern = functools.partial(_attn_kernel, n_chunks=n_chunks, cq=cq, mask_first=mask_first)
    return pl.pallas_call(
        kern,
        grid=grid,
        in_specs=[pl.BlockSpec((nq, d), q_map),
                  kprev_spec,
                  pl.BlockSpec((nq, d), k_map),
                  vprev_spec,
                  pl.BlockSpec((nq, d), v_map),
                  pl.BlockSpec((N_HEADS, cq, span), lambda b, i: (0, 0, 0))],
        out_specs=pl.BlockSpec((nq, d), lambda b, i, g=grid: (b * g[1] + i, 0)),
        out_shape=jax.ShapeDtypeStruct((out_rows, d), _bf16),
        scratch_shapes=[pltpu.VMEM((BAND + nq, d), _bf16), pltpu.VMEM((BAND + nq, d), _bf16)],
        compiler_params=pltpu.CompilerParams(
            dimension_semantics=("parallel", "arbitrary"), vmem_limit_bytes=VMEM_LIMIT),
        name="band_attention",
    )(q_src, kprev_src, q_src, vprev_src, q_src, bias)


def _shift_rows(x, s, fill, row):
    return jnp.where(row >= s, pltpu.roll(x, s, axis=0), fill)


def _lru_kernel(x_ref, gb_ref, cinit_ref, h0_ref, cw_ref, cb_ref, wr_ref, br_ref, wi_ref, bi_ref,
                lam_ref, o_ref, hl_ref, xbuf, hcar, *, tt):
    d = x_ref.shape[-1]

    @pl.when(pl.program_id(1) == 0)
    def _():
        xbuf[0:SUBLANES, :] = cinit_ref[...]
        hcar[...] = h0_ref[...]

    xbuf[SUBLANES:SUBLANES + tt, :] = x_ref[...]
    base = SUBLANES - (CONV_W - 1)
    xc = cb_ref[...]
    for k in range(CONV_W):
        xc = xc + xbuf[pl.ds(base + k, tt), :] * cw_ref[k:k + 1, :]
    xbuf[0:SUBLANES, :] = xbuf[tt:tt + SUBLANES, :]

    xcb = xc.astype(_bf16)
    n_groups = d // MXU_DIM
    r_parts, i_parts = [], []
    for g in range(n_groups):
        xs = xcb[:, g * MXU_DIM:(g + 1) * MXU_DIM]
        r_parts.append(jnp.dot(xs, wr_ref[g], preferred_element_type=_f32))
        i_parts.append(jnp.dot(xs, wi_ref[g], preferred_element_type=_f32))
    r = jax.nn.sigmoid(jnp.concatenate(r_parts, axis=-1) + br_ref[...])
    gi = jax.nn.sigmoid(jnp.concatenate(i_parts, axis=-1) + bi_ref[...])
    z = -lam_ref[...]
    softplus = jnp.maximum(z, 0.0) + jnp.log(1.0 + jnp.exp(-jnp.abs(z)))
    log_a = (-LRU_C) * r * softplus
    a = jnp.exp(log_a)
    b = jnp.sqrt(1.0 - a * a) * (gi * xc)

    row = lax.broadcasted_iota(jnp.int32, (tt, d), 0)
    s = 1
    while s < tt:
        b = a * _shift_rows(b, s, 0.0, row) + b
        a = a * _shift_rows(a, s, 1.0, row)
        s *= 2
    h = a * hcar[...] + b
    h_last = h[tt - 1:tt, :]
    hcar[...] = h_last
    hl_ref[...] = h_last
    o_ref[...] = (h * jax.nn.gelu(gb_ref[...])).astype(o_ref.dtype)


def _lru(proj, conv_init, h0, weights, *, n_batch, seq, tt, row0):
    d = conv_init.shape[-1]
    steps = seq // tt
    blk0 = row0 // tt
    (cw, cb, wr, br, wi, bi, lam) = weights
    const2 = lambda b, i: (0, 0)
    const3 = lambda b, i: (0, 0, 0)
    kern = functools.partial(_lru_kernel, tt=tt)
    return pl.pallas_call(
        kern,
        grid=(n_batch, steps),
        in_specs=[pl.BlockSpec((tt, d), lambda b, i: (blk0 + b * steps + i, 3)),
                  pl.BlockSpec((tt, d), lambda b, i: (blk0 + b * steps + i, 4)),
                  pl.BlockSpec((None, SUBLANES, d), lambda b, i: (b, 0, 0)),
                  pl.BlockSpec((None, 1, d), lambda b, i: (b, 0, 0)),
                  pl.BlockSpec((CONV_W, d), const2),
                  pl.BlockSpec((1, d), const2),
                  pl.BlockSpec(wr.shape, const3),
                  pl.BlockSpec((1, d), const2),
                  pl.BlockSpec(wi.shape, const3),
                  pl.BlockSpec((1, d), const2),
                  pl.BlockSpec((1, d), const2)],
        out_specs=[pl.BlockSpec((tt, d), lambda b, i: (b * steps + i, 0)),
                   pl.BlockSpec((None, 1, d), lambda b, i: (b, 0, 0))],
        out_shape=[jax.ShapeDtypeStruct((n_batch * seq, d), _bf16),
                   jax.ShapeDtypeStruct((n_batch, 1, d), _f32)],
        scratch_shapes=[pltpu.VMEM((tt + 2 * SUBLANES, d), _f32), pltpu.VMEM((1, d), _f32)],
        compiler_params=pltpu.CompilerParams(
            dimension_semantics=("parallel", "arbitrary"), vmem_limit_bytes=VMEM_LIMIT),
        name="conv_rglru",
    )(proj, proj, conv_init, h0, cw, cb, wr, br, wi, bi, lam)


def _merge_kernel(x_ref, oa_ref, ol_ref, ga_ref, gr_ref, wba_ref, wbl_ref, wo_ref, gn_ref, wq_ref,
                  x1_ref, xt_ref, qp_ref):
    ya = jnp.dot(oa_ref[...], wba_ref[...], preferred_element_type=_f32)
    yl = jnp.dot(ol_ref[...], wbl_ref[...], preferred_element_type=_f32)
    merged = jax.nn.sigmoid(ga_ref[...]) * ya + jax.nn.sigmoid(gr_ref[...]) * yl
    x1 = x_ref[...] + jnp.dot(merged.astype(_bf16), wo_ref[...], preferred_element_type=_f32)
    x1_ref[...] = x1
    xt = _rms(x1, gn_ref[...])
    xt_ref[...] = xt
    qp_ref[...] = jnp.dot(xt.astype(_bf16), wq_ref[...], preferred_element_type=_f32).astype(_bf16)


def _merge(x, o_attn, o_lru, proj, w_ba, w_bl, w_o, g_ffn, wq):
    n, d = x.shape
    tm = _row_tile(n)
    nq = wq.shape[1]
    row = lambda i: (i, 0)
    const = lambda i: (0, 0)
    return pl.pallas_call(
        _merge_kernel,
        grid=(n // tm,),
        in_specs=[pl.BlockSpec((tm, d), row),
                  pl.BlockSpec((tm, d), row),
                  pl.BlockSpec((tm, d), row),
                  pl.BlockSpec((tm, d), lambda i: (i, 5)),
                  pl.BlockSpec((tm, d), lambda i: (i, 6)),
                  pl.BlockSpec((d, d), const),
                  pl.BlockSpec((d, d), const),
                  pl.BlockSpec((d, d), const),
                  pl.BlockSpec((1, d), const),
                  pl.BlockSpec((d, nq), const)],
        out_specs=[pl.BlockSpec((tm, d), row), pl.BlockSpec((tm, d), row), pl.BlockSpec((tm, nq), row)],
        out_shape=[jax.ShapeDtypeStruct((n, d), _f32), jax.ShapeDtypeStruct((n, d), _f32),
                   jax.ShapeDtypeStruct((n, nq), _bf16)],
        compiler_params=pltpu.CompilerParams(
            dimension_semantics=("parallel",), vmem_limit_bytes=VMEM_LIMIT),
        name="merge_outproj_peerq",
    )(x, o_attn, o_lru, proj, proj, w_ba, w_bl, w_o, g_ffn, wq)


def _top16(s, payload=None):
    n_rows = s.shape[0]
    row = lax.broadcasted_iota(jnp.int32, s.shape, 0)
    vals, picks = [], []
    for _ in range(TOPK):
        m = jnp.max(s, axis=0, keepdims=True)
        am = jnp.min(jnp.where(s == m, row, n_rows), axis=0, keepdims=True)
        hit = row == am
        vals.append(m)
        if payload is None:
            picks.append(am)
        else:
            picks.append(jnp.max(jnp.where(hit, payload, -1), axis=0, keepdims=True))
        s = jnp.where(hit, -jnp.inf, s)
    return jnp.concatenate(vals, axis=0), jnp.concatenate(picks, axis=0)


def _topk_kernel(qp_ref, k1_ref, k2_ref, idx_ref, g_ref):
    nt = (((1,), (1,)), ((), ()))
    for h in range(PEER_HEADS):
        c0 = h * 2 * N_KEYS
        q1 = qp_ref[:, c0:c0 + N_KEYS]
        q2 = qp_ref[:, c0 + N_KEYS:c0 + 2 * N_KEYS]
        s1 = lax.dot_general(k1_ref[h], q1, nt, preferred_element_type=_f32)
        s2 = lax.dot_general(k2_ref[h], q2, nt, preferred_element_type=_f32)
        v1, i1 = _top16(s1)
        v2, i2 = _top16(s2)
        cand_s = jnp.concatenate([v1[i:i + 1, :] + v2 for i in range(TOPK)], axis=0)
        cand_i = jnp.concatenate([i1[i:i + 1, :] * N_KEYS + i2 for i in range(TOPK)], axis=0)
        top_s, e_idx = _top16(cand_s, cand_i)
        p = jnp.exp(top_s - jnp.max(top_s, axis=0, keepdims=True))
        g = p / jnp.sum(p, axis=0, keepdims=True)
        idx_ref[h * TOPK:(h + 1) * TOPK, :] = e_idx
        g_ref[h * TOPK:(h + 1) * TOPK, :] = g


def _topk(qp, keys1, keys2):
    n = qp.shape[0]
    nsel = PEER_HEADS * TOPK
    const3 = lambda i: (0, 0, 0)
    return pl.pallas_call(
        _topk_kernel,
        grid=(n // TOKEN_TILE,),
        in_specs=[pl.BlockSpec((TOKEN_TILE, qp.shape[1]), lambda i: (i, 0)),
                  pl.BlockSpec(keys1.shape, const3),
                  pl.BlockSpec(keys2.shape, const3)],
        out_specs=[pl.BlockSpec((nsel, TOKEN_TILE), lambda i: (0, i)),
                   pl.BlockSpec((nsel, TOKEN_TILE), lambda i: (0, i))],
        out_shape=[jax.ShapeDtypeStruct((nsel, n), jnp.int32), jax.ShapeDtypeStruct((nsel, n), _f32)],
        compiler_params=pltpu.CompilerParams(
            dimension_semantics=("parallel",), vmem_limit_bytes=VMEM_LIMIT),
        name="peer_topk",
    )(qp, keys1, keys2)


def _pack_table(t):
    e, d = t.shape
    bits = lax.bitcast_convert_type(t.astype(_bf16), jnp.uint16).astype(jnp.uint32)
    words = bits[:, :d // 2] | (bits[:, d // 2:] << 16)
    return words.reshape(e * SLAB, LANES)


def _unpack(words):
    lo = pltpu.bitcast(words << 16, _f32)
    hi = pltpu.bitcast(words & jnp.uint32(0xFFFF0000), _f32)
    return lo, hi


def _load_table(tab_hbm, tab, sem):
    @pl.when(pl.program_id(0) == 0)
    def _():
        cp = pltpu.make_async_copy(tab_hbm, tab, sem)
        cp.start()
        cp.wait()


def _peer_u_kernel(idx_ref, xt_ref, g_ref, tab_hbm, w_ref, tab, tile, sem):
    _load_table(tab_hbm, tab, sem)
    nsel = idx_ref.shape[1]
    lane = lax.broadcasted_iota(jnp.int32, (nsel, TOKEN_TILE), 1)

    def token(t, act_t):
        for j in range(nsel):
            r = pl.multiple_of(idx_ref[t, j] * SLAB, SLAB)
            tile[pl.ds(j, SLAB, stride=SLAB_STRIDE), :] = tab[pl.ds(r, SLAB), :]
        acc = jnp.zeros((nsel, LANES), _f32)
        x = xt_ref[t]
        for s in range(SLAB):
            lo, hi = _unpack(tile[s * SLAB_STRIDE:s * SLAB_STRIDE + nsel, :])
            acc = acc + lo * x[s:s + 1, :] + hi * x[SLAB + s:SLAB + s + 1, :]
        act = jnp.sum(acc, axis=1, keepdims=True)
        return jnp.where(lane == t, act, act_t)

    act_t = lax.fori_loop(0, TOKEN_TILE, token, jnp.zeros((nsel, TOKEN_TILE), _f32))
    w_ref[...] = g_ref[...] * jax.nn.gelu(act_t)


def _peer_u(idx, xt, g_t, table):
    n, d = xt.shape
    nsel = idx.shape[1]
    xt = xt.reshape(n, d // LANES, LANES)
    return pl.pallas_call(
        _peer_u_kernel,
        grid=(n // TOKEN_TILE,),
        in_specs=[pl.BlockSpec((TOKEN_TILE, nsel), lambda i: (i, 0), memory_space=pltpu.SMEM),
                  pl.BlockSpec((TOKEN_TILE, d // LANES, LANES), lambda i: (i, 0, 0)),
                  pl.BlockSpec((nsel, TOKEN_TILE), lambda i: (0, i)),
                  pl.BlockSpec(memory_space=pl.ANY)],
        out_specs=pl.BlockSpec((nsel, TOKEN_TILE), lambda i: (0, i)),
        out_shape=jax.ShapeDtypeStruct((nsel, n), _f32),
        scratch_shapes=[pltpu.VMEM(table.shape, jnp.uint32),
                        pltpu.VMEM((SLAB * SLAB_STRIDE, LANES), jnp.uint32),
                        pltpu.SemaphoreType.DMA(())],
        compiler_params=pltpu.CompilerParams(
            dimension_semantics=("arbitrary",), vmem_limit_bytes=VMEM_LIMIT),
        name="peer_u_pass",
    )(idx, xt, g_t, table)


def _peer_v_kernel(idx_ref, w_ref, tab_hbm, o_ref, tab, sem):
    _load_table(tab_hbm, tab, sem)
    nsel = idx_ref.shape[1]
    n_acc = 4

    def token(t, carry):
        acc_lo = [jnp.zeros((SLAB, LANES), _f32) for _ in range(n_acc)]
        acc_hi = [jnp.zeros((SLAB, LANES), _f32) for _ in range(n_acc)]
        for j in range(nsel):
            r = pl.multiple_of(idx_ref[t, j] * SLAB, SLAB)
            lo, hi = _unpack(tab[pl.ds(r, SLAB), :])
            wj = w_ref[t, j]
            acc_lo[j % n_acc] = acc_lo[j % n_acc] + wj * lo
            acc_hi[j % n_acc] = acc_hi[j % n_acc] + wj * hi
        lo = (acc_lo[0] + acc_lo[1]) + (acc_lo[2] + acc_lo[3])
        hi = (acc_hi[0] + acc_hi[1]) + (acc_hi[2] + acc_hi[3])
        o_ref[t] = jnp.concatenate([lo, hi], axis=0)
        return carry

    lax.fori_loop(0, TOKEN_TILE, token, 0)


def _peer_v(idx, w, table, d):
    n, nsel = idx.shape
    smem_rows = pl.BlockSpec((TOKEN_TILE, nsel), lambda i: (i, 0), memory_space=pltpu.SMEM)
    return pl.pallas_call(
        _peer_v_kernel,
        grid=(n // TOKEN_TILE,),
        in_specs=[smem_rows, smem_rows, pl.BlockSpec(memory_space=pl.ANY)],
        out_specs=pl.BlockSpec((TOKEN_TILE, d // LANES, LANES), lambda i: (i, 0, 0)),
        out_shape=jax.ShapeDtypeStruct((n, d // LANES, LANES), _f32),
        scratch_shapes=[pltpu.VMEM(table.shape, jnp.uint32), pltpu.SemaphoreType.DMA(())],
        compiler_params=pltpu.CompilerParams(
            dimension_semantics=("arbitrary",), vmem_limit_bytes=VMEM_LIMIT),
        name="peer_v_pass",
    )(idx, w, table)


def _final_kernel(x1_ref, p_ref, g_ref, y_ref):
    y_ref[...] = _rms(x1_ref[...] + p_ref[...], g_ref[...])


def _final(x1, peer, g):
    n, d = x1.shape
    tm = _row_tile(n)
    row = lambda i: (i, 0)
    return pl.pallas_call(
        _final_kernel,
        grid=(n // tm,),
        in_specs=[pl.BlockSpec((tm, d), row), pl.BlockSpec((tm, d), row), pl.BlockSpec((1, d), lambda i: (0, 0))],
        out_specs=pl.BlockSpec((tm, d), row),
        out_shape=jax.ShapeDtypeStruct((n, d), _f32),
        compiler_params=pltpu.CompilerParams(
            dimension_semantics=("parallel",), vmem_limit_bytes=VMEM_LIMIT),
        name="residual_final_norm",
    )(x1, peer, g)


def _rel_bias(rel_table, q_pos, k_pos):
    idx = jnp.clip(q_pos[:, None] - k_pos[None, :], -MAX_REL, MAX_REL) + MAX_REL
    return rel_table[:, idx].astype(_f32)


def _block_diag_groups(w):
    nb, bi, bo = w.shape
    per = MXU_DIM // bi
    w = w.reshape(nb // per, per, bi, bo)
    eye = jnp.eye(per, dtype=w.dtype)
    return jnp.einsum("gpio,pq->gpiqo", w, eye).reshape(nb // per, per * bi, per * bo)


def _layer(xp, xs, cache_k, cache_v, state_conv, state_lru, norm_mix, w_in, rel_table, conv_w, conv_b,
           lru_wr, lru_br, lru_wi, lru_bi, lru_lambda, w_ba, w_bl, w_o, norm_ffn, peer_wq, keys1, keys2,
           tab_u, tab_v):
    bsz, seq, d = xp.shape
    dbsz, dseq, _ = xs.shape
    n_p, n_s = bsz * seq, dbsz * dseq
    x_all = jnp.concatenate([xp.reshape(n_p, d), xs.reshape(n_s, d)], axis=0)
    row = lambda v: v.reshape(1, -1)

    proj = _inproj(x_all, row(norm_mix), w_in.astype(_bf16))

    tiles = seq // BAND
    bias_p = _rel_bias(rel_table, jnp.arange(CHUNK) + BAND, jnp.arange(BAND + CHUNK))
    prev = lambda c: (lambda b, i: (b * tiles + jnp.maximum(i - 1, 0), c))
    cur = lambda c: (lambda b, i: (b * tiles + i, c))
    o_attn_p = _attention(
        proj, proj, proj, bias_p, out_rows=n_p, n_chunks=BAND // CHUNK, cq=CHUNK, mask_first=True,
        grid=(bsz, tiles), q_map=cur(0), k_map=cur(1), v_map=cur(2),
        kprev_spec=pl.BlockSpec((BAND, d), prev(1)), vprev_spec=pl.BlockSpec((BAND, d), prev(2)))
    rows_c = cache_k.shape[1]
    bias_s = _rel_bias(rel_table, PAST_LEN + jnp.arange(dseq), PAST_LEN - rows_c + jnp.arange(rows_c + dseq))
    blk_s = n_p // dseq
    samp = lambda c: (lambda b, i: (blk_s + b, c))
    cache_spec = pl.BlockSpec((None, rows_c, d), lambda b, i: (b, 0, 0))
    o_attn_s = _attention(
        proj, cache_k.reshape(dbsz, rows_c, d), cache_v.reshape(dbsz, rows_c, d), bias_s,
        out_rows=n_s, n_chunks=1, cq=dseq, mask_first=False, grid=(dbsz, 1),
        q_map=samp(0), k_map=samp(1), v_map=samp(2), kprev_spec=cache_spec, vprev_spec=cache_spec)

    lru_w = (conv_w, row(conv_b), _block_diag_groups(lru_wr).astype(_bf16), row(lru_br),
             _block_diag_groups(lru_wi).astype(_bf16), row(lru_bi), row(lru_lambda))
    pad_state = lambda st: jnp.pad(st, ((0, 0), (SUBLANES - (CONV_W - 1), 0), (0, 0)))
    o_lru_p, h_p = _lru(proj, jnp.zeros((bsz, SUBLANES, d), _f32), jnp.zeros((bsz, 1, d), _f32), lru_w,
                        n_batch=bsz, seq=seq, tt=256, row0=0)
    o_lru_s, h_s = _lru(proj, pad_state(state_conv), state_lru.reshape(dbsz, 1, d), lru_w,
                        n_batch=dbsz, seq=dseq, tt=dseq, row0=n_p)

    o_attn = jnp.concatenate([o_attn_p, o_attn_s], axis=0)
    o_lru = jnp.concatenate([o_lru_p, o_lru_s], axis=0)
    x1, xt, qp = _merge(x_all, o_attn, o_lru, proj, w_ba.astype(_bf16), w_bl.astype(_bf16),
                        w_o.astype(_bf16), row(norm_ffn), peer_wq.astype(_bf16))

    idx_t, g_t = _topk(qp, keys1.astype(_bf16), keys2.astype(_bf16))
    idx = idx_t.T
    w_t = _peer_u(idx, xt, g_t, tab_u)
    peer = _peer_v(idx, w_t.T, tab_v, d).reshape(n_p + n_s, d)

    k_p = proj[:n_p, d:2 * d].reshape(bsz, seq, N_HEADS, HEAD_DIM)
    v_p = proj[:n_p, 2 * d:3 * d].reshape(bsz, seq, N_HEADS, HEAD_DIM)
    xr_p = proj[:n_p, 3 * d:4 * d].reshape(bsz, seq, d)
    k_s = proj[n_p:, d:2 * d].reshape(dbsz, dseq, N_HEADS, HEAD_DIM)
    v_s = proj[n_p:, 2 * d:3 * d].reshape(dbsz, dseq, N_HEADS, HEAD_DIM)
    xr_s = jnp.concatenate([state_conv, proj[n_p:, 3 * d:4 * d].reshape(dbsz, dseq, d)], axis=1)
    rows_p = min(BAND, seq)
    keep = CONV_W - 1
    return (x1, peer, k_p[:, seq - rows_p:], v_p[:, seq - rows_p:], xr_p[:, seq - keep:], h_p.reshape(bsz, d),
            k_s, v_s, xr_s[:, xr_s.shape[1] - keep:], h_s.reshape(dbsz, d))


def kernel(x_prompt, x_sample, cache_k, cache_v, state_conv, state_lru, norm_mix, w_in, rel_table, conv_w, conv_b, lru_wr, lru_br, lru_wi, lru_bi, lru_lambda, w_branch_attn, w_branch_lru, w_out, norm_ffn, peer_wq, peer_keys1, peer_keys2, peer_u, peer_v, norm_final):
    depth = w_in.shape[0]
    assert depth == 1, "the stacked-token pipeline is written for a single layer"
    bsz, seq, d = x_prompt.shape
    dbsz, dseq, _ = x_sample.shape
    n_p = bsz * seq
    l = 0
    (x1, peer, k_p, v_p, c_p, h_p, k_s, v_s, c_s, h_s) = _layer(
        x_prompt, x_sample, cache_k[l], cache_v[l], state_conv[l], state_lru[l], norm_mix[l], w_in[l],
        rel_table[l], conv_w[l], conv_b[l], lru_wr[l], lru_br[l], lru_wi[l], lru_bi[l], lru_lambda[l],
        w_branch_attn[l], w_branch_lru[l], w_out[l], norm_ffn[l], peer_wq[l], peer_keys1[l], peer_keys2[l],
        _pack_table(peer_u[l]), _pack_table(peer_v[l]))
    y = _final(x1, peer, norm_final.reshape(1, -1))
    y_prompt = y[:n_p].reshape(bsz, seq, d)
    y_sample = y[n_p:].reshape(dbsz, dseq, d)
    stack = lambda v: v[None]
    return (y_prompt, y_sample, stack(k_p), stack(v_p), stack(c_p), stack(h_p),
            stack(k_s), stack(v_s), stack(c_s), stack(h_s))
```

```python
import functools

import jax
import jax.numpy as jnp
from jax import lax
from jax.experimental import pallas as pl
from jax.experimental.pallas import tpu as pltpu

CHUNK = 64
LEFT_CHUNKS = 8
BAND = LEFT_CHUNKS * CHUNK
N_HEADS = 16
HEAD_DIM = 64
ATTN_SCALE = HEAD_DIM ** -0.5
MAX_REL = 128
PAST_LEN = 2048
LRU_BLOCKS = 16
CONV_W = 4
LRU_C = 8.0
PEER_HEADS = 8
N_KEYS = 128
TOPK = 16
EPS = 1e-6
NEG_INF = -1e30

LANES = 128
SUBLANES = 8
MXU_DIM = 256
TOKEN_TILE = 128
SLAB = 4
SLAB_STRIDE = 136
VMEM_LIMIT = 56 * 1024 * 1024

_bf16 = jnp.bfloat16
_f32 = jnp.float32


def _row_tile(n_rows):
    for t in (512, 384, 256, 128):
        if n_rows % t == 0:
            return t
    raise ValueError(f"token count {n_rows} must be a multiple of 128")


def _rms(x, g):
    return x * lax.rsqrt(jnp.mean(x * x, axis=-1, keepdims=True) + EPS) * g


def _inproj_kernel(x_ref, g_ref, w_ref, o_ref, xn_ref):
    @pl.when(pl.program_id(1) == 0)
    def _():
        xn_ref[...] = _rms(x_ref[...], g_ref[...]).astype(_bf16)

    o_ref[...] = jnp.dot(xn_ref[...], w_ref[...], preferred_element_type=_f32)


def _inproj(x, g, w):
    n, d = x.shape
    cols = w.shape[1]
    tm = _row_tile(n)
    return pl.pallas_call(
        _inproj_kernel,
        grid=(n // tm, cols // d),
        in_specs=[pl.BlockSpec((tm, d), lambda i, j: (i, 0)),
                  pl.BlockSpec((1, d), lambda i, j: (0, 0)),
                  pl.BlockSpec((d, d), lambda i, j: (0, j))],
        out_specs=pl.BlockSpec((tm, d), lambda i, j: (i, j)),
        out_shape=jax.ShapeDtypeStruct((n, cols), _f32),
        scratch_shapes=[pltpu.VMEM((tm, d), _bf16)],
        compiler_params=pltpu.CompilerParams(
            dimension_semantics=("parallel", "arbitrary"), vmem_limit_bytes=VMEM_LIMIT),
        name="inproj",
    )(x, g, w)


def _attn_kernel(q_ref, kp_ref, kc_ref, vp_ref, vc_ref, bias_ref, o_ref, kcat, vcat,
                 *, n_chunks, cq, mask_first):
    nq = n_chunks * cq
    span = BAND + cq
    kcat[0:BAND, :] = kp_ref[...].astype(_bf16)
    kcat[BAND:BAND + nq, :] = kc_ref[...].astype(_bf16)
    vcat[0:BAND, :] = vp_ref[...].astype(_bf16)
    vcat[BAND:BAND + nq, :] = vc_ref[...].astype(_bf16)
    first = pl.program_id(1) == 0
    pair = 2 * HEAD_DIM
    low_head = lax.broadcasted_iota(jnp.int32, (cq, pair), 1) < HEAD_DIM

    def chunk(cl, carry):
        r0 = pl.multiple_of(cl * cq, cq)
        q = q_ref[pl.ds(r0, cq), :] * ATTN_SCALE
        kw = kcat[pl.ds(r0, span), :]
        vw = vcat[pl.ds(r0, span), :]
        if mask_first:
            col = lax.broadcasted_iota(jnp.int32, (cq, span), 1)
            valid = jnp.logical_or(jnp.logical_not(first), col + r0 >= BAND)
        scores = []
        for pr in range(N_HEADS // 2):
            sl = slice(pr * pair, (pr + 1) * pair)
            for keep_low in (True, False):
                qh = (jnp.where(low_head, q[:, sl], 0.0) if keep_low
                      else jnp.where(low_head, 0.0, q[:, sl])).astype(_bf16)
                scores.append(lax.dot_general(qh, kw[:, sl], (((1,), (1,)), ((), ())),
                                              preferred_element_type=_f32))
        probs, inv = [], []
        for h in range(N_HEADS):
            s = scores[h] + bias_ref[h]
            if mask_first:
                s = jnp.where(valid, s, NEG_INF)
            p = jnp.exp(s - jnp.max(s, axis=-1, keepdims=True))
            inv.append(1.0 / jnp.sum(p, axis=-1, keepdims=True))
            probs.append(p.astype(_bf16))
        outs = []
        for pr in range(N_HEADS // 2):
            vp = vw[:, pr * pair:(pr + 1) * pair]
            o_lo = jnp.dot(probs[2 * pr], vp, preferred_element_type=_f32) * inv[2 * pr]
            o_hi = jnp.dot(probs[2 * pr + 1], vp, preferred_element_type=_f32) * inv[2 * pr + 1]
            outs.append(jnp.where(low_head, o_lo, o_hi))
        o_ref[pl.ds(r0, cq), :] = jnp.concatenate(outs, axis=-1).astype(o_ref.dtype)
        return carry

    lax.fori_loop(0, n_chunks, chunk, 0)


def _attention(q_src, kprev_src, vprev_src, bias, *, out_rows, n_chunks, cq, mask_first, grid,
               q_map, k_map, v_map, kprev_spec, vprev_spec):
    nq = n_chunks * cq
    d = N_HEADS * HEAD_DIM
    span = BAND + cq
    kern = functools.partial(_attn_kernel, n_chunks=n_chunks, cq=cq, mask_first=mask_first)
    return pl.pallas_call(
        kern,
        grid=grid,
        in_specs=[pl.BlockSpec((nq, d), q_map),
                  kprev_spec,
                  pl.BlockSpec((nq, d), k_map),
                  vprev_spec,
                  pl.BlockSpec((nq, d), v_map),
                  pl.BlockSpec((N_HEADS, cq, span), lambda b, i: (0, 0, 0))],
        out_specs=pl.BlockSpec((nq, d), lambda b, i, g=grid: (b * g[1] + i, 0)),
        out_shape=jax.ShapeDtypeStruct((out_rows, d), _bf16),
        scratch_shapes=[pltpu.VMEM((BAND + nq, d), _bf16), pltpu.VMEM((BAND + nq, d), _bf16)],
        compiler_params=pltpu.CompilerParams(
            dimension_semantics=("parallel", "arbitrary"), vmem_limit_bytes=VMEM_LIMIT),
        name="band_attention",
    )(q_src, kprev_src, q_src, vprev_src, q_src, bias)


def _shift_rows(x, s, fill, row):
    return jnp.where(row >= s, pltpu.roll(x, s, axis=0), fill)


def _lru_kernel(x_ref, gb_ref, cinit_ref, h0_ref, cw_ref, cb_ref, wr_ref, br_ref, wi_ref, bi_ref,
                lam_ref, o_ref, hl_ref, xbuf, hcar, *, tt):
    d = x_ref.shape[-1]

    @pl.when(pl.program_id(1) == 0)
    def _():
        xbuf[0:SUBLANES, :] = cinit_ref[...]
        hcar[...] = h0_ref[...]

    xbuf[SUBLANES:SUBLANES + tt, :] = x_ref[...]
    base = SUBLANES - (CONV_W - 1)
    xc = cb_ref[...]
    for k in range(CONV_W):
        xc = xc + xbuf[pl.ds(base + k, tt), :] * cw_ref[k:k + 1, :]
    xbuf[0:SUBLANES, :] = xbuf[tt:tt + SUBLANES, :]

    xcb = xc.astype(_bf16)
    n_groups = d // MXU_DIM
    r_parts, i_parts = [], []
    for g in range(n_groups):
        xs = xcb[:, g * MXU_DIM:(g + 1) * MXU_DIM]
        r_parts.append(jnp.dot(xs, wr_ref[g], preferred_element_type=_f32))
        i_parts.append(jnp.dot(xs, wi_ref[g], preferred_element_type=_f32))
    r = jax.nn.sigmoid(jnp.concatenate(r_parts, axis=-1) + br_ref[...])
    gi = jax.nn.sigmoid(jnp.concatenate(i_parts, axis=-1) + bi_ref[...])
    z = -lam_ref[...]
    softplus = jnp.maximum(z, 0.0) + jnp.log(1.0 + jnp.exp(-jnp.abs(z)))
    log_a = (-LRU_C) * r * softplus
    a = jnp.exp(log_a)
    b = jnp.sqrt(1.0 - a * a) * (gi * xc)

    row = lax.broadcasted_iota(jnp.int32, (tt, d), 0)
    s = 1
    while s < tt:
        b = a * _shift_rows(b, s, 0.0, row) + b
        a = a * _shift_rows(a, s, 1.0, row)
        s *= 2
    h = a * hcar[...] + b
    h_last = h[tt - 1:tt, :]
    hcar[...] = h_last
    hl_ref[...] = h_last
    o_ref[...] = (h * jax.nn.gelu(gb_ref[...])).astype(o_ref.dtype)


def _lru(proj, conv_init, h0, weights, *, n_batch, seq, tt, row0):
    d = conv_init.shape[-1]
    steps = seq // tt
    blk0 = row0 // tt
    (cw, cb, wr, br, wi, bi, lam) = weights
    const2 = lambda b, i: (0, 0)
    const3 = lambda b, i: (0, 0, 0)
    kern = functools.partial(_lru_kernel, tt=tt)
    return pl.pallas_call(
        kern,
        grid=(n_batch, steps),
        in_specs=[pl.BlockSpec((tt, d), lambda b, i: (blk0 + b * steps + i, 3)),
                  pl.BlockSpec((tt, d), lambda b, i: (blk0 + b * steps + i, 4)),
                  pl.BlockSpec((None, SUBLANES, d), lambda b, i: (b, 0, 0)),
                  pl.BlockSpec((None, 1, d), lambda b, i: (b, 0, 0)),
                  pl.BlockSpec((CONV_W, d), const2),
                  pl.BlockSpec((1, d), const2),
                  pl.BlockSpec(wr.shape, const3),
                  pl.BlockSpec((1, d), const2),
                  pl.BlockSpec(wi.shape, const3),
                  pl.BlockSpec((1, d), const2),
                  pl.BlockSpec((1, d), const2)],
        out_specs=[pl.BlockSpec((tt, d), lambda b, i: (b * steps + i, 0)),
                   pl.BlockSpec((None, 1, d), lambda b, i: (b, 0, 0))],
        out_shape=[jax.ShapeDtypeStruct((n_batch * seq, d), _bf16),
                   jax.ShapeDtypeStruct((n_batch, 1, d), _f32)],
        scratch_shapes=[pltpu.VMEM((tt + 2 * SUBLANES, d), _f32), pltpu.VMEM((1, d), _f32)],
        compiler_params=pltpu.CompilerParams(
            dimension_semantics=("parallel", "arbitrary"), vmem_limit_bytes=VMEM_LIMIT),
        name="conv_rglru",
    )(proj, proj, conv_init, h0, cw, cb, wr, br, wi, bi, lam)


def _merge_kernel(x_ref, oa_ref, ol_ref, ga_ref, gr_ref, wba_ref, wbl_ref, wo_ref, gn_ref, wq_ref,
                  x1_ref, xt_ref, qp_ref):
    ya = jnp.dot(oa_ref[...], wba_ref[...], preferred_element_type=_f32)
    yl = jnp.dot(ol_ref[...], wbl_ref[...], preferred_element_type=_f32)
    merged = jax.nn.sigmoid(ga_ref[...]) * ya + jax.nn.sigmoid(gr_ref[...]) * yl
    x1 = x_ref[...] + jnp.dot(merged.astype(_bf16), wo_ref[...], preferred_element_type=_f32)
    x1_ref[...] = x1
    xt = _rms(x1, gn_ref[...])
    xt_ref[...] = xt
    qp_ref[...] = jnp.dot(xt.astype(_bf16), wq_ref[...], preferred_element_type=_f32).astype(_bf16)


def _merge(x, o_attn, o_lru, proj, w_ba, w_bl, w_o, g_ffn, wq):
    n, d = x.shape
    tm = _row_tile(n)
    nq = wq.shape[1]
    row = lambda i: (i, 0)
    const = lambda i: (0, 0)
    return pl.pallas_call(
        _merge_kernel,
        grid=(n // tm,),
        in_specs=[pl.BlockSpec((tm, d), row),
                  pl.BlockSpec((tm, d), row),
                  pl.BlockSpec((tm, d), row),
                  pl.BlockSpec((tm, d), lambda i: (i, 5)),
                  pl.BlockSpec((tm, d), lambda i: (i, 6)),
                  pl.BlockSpec((d, d), const),
                  pl.BlockSpec((d, d), const),
                  pl.BlockSpec((d, d), const),
                  pl.BlockSpec((1, d), const),
                  pl.BlockSpec((d, nq), const)],
        out_specs=[pl.BlockSpec((tm, d), row), pl.BlockSpec((tm, d), row), pl.BlockSpec((tm, nq), row)],
        out_shape=[jax.ShapeDtypeStruct((n, d), _f32), jax.ShapeDtypeStruct((n, d), _f32),
                   jax.ShapeDtypeStruct((n, nq), _bf16)],
        compiler_params=pltpu.CompilerParams(
            dimension_semantics=("parallel",), vmem_limit_bytes=VMEM_LIMIT),
        name="merge_outproj_peerq",
    )(x, o_attn, o_lru, proj, proj, w_ba, w_bl, w_o, g_ffn, wq)


def _top16(s, payload=None):
    n_rows = s.shape[0]
    row = lax.broadcasted_iota(jnp.int32, s.shape, 0).astype(_f32)
    vals, picks = [], []
    for _ in range(TOPK):
        m = jnp.max(s, axis=0, keepdims=True)
        am = jnp.min(jnp.where(s == m, row, float(n_rows)), axis=0, keepdims=True)
        hit = row == am
        vals.append(m)
        if payload is None:
            picks.append(am)
        else:
            picks.append(jnp.max(jnp.where(hit, payload, -1.0), axis=0, keepdims=True))
        s = jnp.where(hit, -jnp.inf, s)
    return jnp.concatenate(vals, axis=0), jnp.concatenate(picks, axis=0)


def _candidates(a, b, mask_out):
    j8 = lax.broadcasted_iota(jnp.int32, (SUBLANES, a.shape[1]), 0)
    pieces = [a[0:1, :] + b]
    for i in range(1, SUBLANES):
        piece = a[i:i + 1, :] + b[0:SUBLANES, :]
        if mask_out:
            piece = jnp.where(j8 < TOPK // (i + 1), piece, -jnp.inf)
        pieces.append(piece)
    pieces.append(a[SUBLANES:TOPK, :] + b[0:1, :])
    return jnp.concatenate(pieces, axis=0)


def _topk_kernel(qp_ref, k1_ref, k2_ref, idx_ref, g_ref):
    nt = (((1,), (1,)), ((), ()))
    for h in range(PEER_HEADS):
        c0 = h * 2 * N_KEYS
        q1 = qp_ref[:, c0:c0 + N_KEYS]
        q2 = qp_ref[:, c0 + N_KEYS:c0 + 2 * N_KEYS]
        s1 = lax.dot_general(k1_ref[h], q1, nt, preferred_element_type=_f32)
        s2 = lax.dot_general(k2_ref[h], q2, nt, preferred_element_type=_f32)
        v1, i1 = _top16(s1)
        v2, i2 = _top16(s2)
        cand_s = _candidates(v1, v2, True)
        cand_i = _candidates(i1 * float(N_KEYS), i2, False)
        top_s, e_idx = _top16(cand_s, cand_i)
        p = jnp.exp(top_s - jnp.max(top_s, axis=0, keepdims=True))
        g = p / jnp.sum(p, axis=0, keepdims=True)
        idx_ref[h * TOPK:(h + 1) * TOPK, :] = e_idx.astype(jnp.int32) * SLAB
        g_ref[h * TOPK:(h + 1) * TOPK, :] = g


def _topk(qp, keys1, keys2):
    n = qp.shape[0]
    nsel = PEER_HEADS * TOPK
    const3 = lambda i: (0, 0, 0)
    return pl.pallas_call(
        _topk_kernel,
        grid=(n // TOKEN_TILE,),
        in_specs=[pl.BlockSpec((TOKEN_TILE, qp.shape[1]), lambda i: (i, 0)),
                  pl.BlockSpec(keys1.shape, const3),
                  pl.BlockSpec(keys2.shape, const3)],
        out_specs=[pl.BlockSpec((nsel, TOKEN_TILE), lambda i: (0, i)),
                   pl.BlockSpec((nsel, TOKEN_TILE), lambda i: (0, i))],
        out_shape=[jax.ShapeDtypeStruct((nsel, n), jnp.int32), jax.ShapeDtypeStruct((nsel, n), _f32)],
        compiler_params=pltpu.CompilerParams(
            dimension_semantics=("parallel",), vmem_limit_bytes=VMEM_LIMIT),
        name="peer_topk",
    )(qp, keys1, keys2)


def _pack_table(t):
    e, d = t.shape
    bits = lax.bitcast_convert_type(t.astype(_bf16), jnp.uint16).astype(jnp.uint32)
    words = bits[:, :d // 2] | (bits[:, d // 2:] << 16)
    return words.reshape(e * SLAB, LANES)


def _unpack(words):
    lo = pltpu.bitcast(words << 16, _f32)
    hi = pltpu.bitcast(words & jnp.uint32(0xFFFF0000), _f32)
    return lo, hi


def _gather_transposed(idx_ref, t, tab_ref, tile):
    for j in range(idx_ref.shape[1]):
        r = pl.multiple_of(idx_ref[t, j], SLAB)
        tile[pl.ds(j, SLAB, stride=SLAB_STRIDE), :] = tab_ref[pl.ds(r, SLAB), :]


def _tile_chunk(tile, s, nsel):
    return _unpack(tile[s * SLAB_STRIDE:s * SLAB_STRIDE + nsel, :])


def _peer_u_kernel(idx_ref, xt_ref, g_ref, tab_ref, w_ref, tile_a, tile_b, act_ref):
    nsel = idx_ref.shape[1]
    lane = lax.broadcasted_iota(jnp.int32, (nsel, TOKEN_TILE), 1)
    act_ref[...] = jnp.zeros_like(act_ref)

    def partial_dots(t, tile):
        _gather_transposed(idx_ref, t, tab_ref, tile)
        x = xt_ref[t]
        acc = jnp.zeros((nsel, LANES), _f32)
        for s in range(SLAB):
            lo, hi = _tile_chunk(tile, s, nsel)
            acc = acc + lo * x[s:s + 1, :] + hi * x[SLAB + s:SLAB + s + 1, :]
        return acc

    def commit(t, acc):
        act = jnp.sum(acc, axis=1, keepdims=True)
        act_ref[...] = jnp.where(lane == t, act, act_ref[...])

    def two_tokens(i, acc_prev):
        t = 2 * i
        commit(t - 1, acc_prev)
        acc_a = partial_dots(t, tile_a)
        acc_b = partial_dots(t + 1, tile_b)
        commit(t, acc_a)
        return acc_b

    acc_last = lax.fori_loop(0, TOKEN_TILE // 2, two_tokens, jnp.zeros((nsel, LANES), _f32))
    commit(TOKEN_TILE - 1, acc_last)
    w_ref[...] = g_ref[...] * jax.nn.gelu(act_ref[...])


def _peer_u(idx, xt, g_t, table):
    n, d = xt.shape
    nsel = idx.shape[1]
    xt = xt.reshape(n, d // LANES, LANES)
    return pl.pallas_call(
        _peer_u_kernel,
        grid=(n // TOKEN_TILE,),
        in_specs=[pl.BlockSpec((TOKEN_TILE, nsel), lambda i: (i, 0), memory_space=pltpu.SMEM),
                  pl.BlockSpec((TOKEN_TILE, d // LANES, LANES), lambda i: (i, 0, 0)),
                  pl.BlockSpec((nsel, TOKEN_TILE), lambda i: (0, i)),
                  _resident_table_spec(table)],
        out_specs=pl.BlockSpec((nsel, TOKEN_TILE), lambda i: (0, i)),
        out_shape=jax.ShapeDtypeStruct((nsel, n), _f32),
        scratch_shapes=[pltpu.VMEM((SLAB * SLAB_STRIDE, LANES), jnp.uint32),
                        pltpu.VMEM((SLAB * SLAB_STRIDE, LANES), jnp.uint32),
                        pltpu.VMEM((nsel, TOKEN_TILE), _f32)],
        compiler_params=pltpu.CompilerParams(
            dimension_semantics=("arbitrary",), vmem_limit_bytes=VMEM_LIMIT),
        name="peer_u_pass",
    )(idx, xt, g_t, table)


def _resident_table_spec(table):
    return pl.BlockSpec(table.shape, lambda i: (0, 0), pipeline_mode=pl.Buffered(1))


def _peer_v_kernel(idx_ref, w_ref, tab_ref, o_ref, tile_a, tile_b):
    nsel = idx_ref.shape[1]
    lane = lax.broadcasted_iota(jnp.int32, (nsel, TOKEN_TILE), 1)

    def token(t, tile):
        _gather_transposed(idx_ref, t, tab_ref, tile)
        w_col = jnp.sum(jnp.where(lane == t, w_ref[...], 0.0), axis=1, keepdims=True)
        rows_lo, rows_hi = [], []
        for s in range(SLAB):
            lo, hi = _tile_chunk(tile, s, nsel)
            rows_lo.append(jnp.sum(lo * w_col, axis=0, keepdims=True))
            rows_hi.append(jnp.sum(hi * w_col, axis=0, keepdims=True))
        o_ref[t] = jnp.concatenate(rows_lo + rows_hi, axis=0)

    def two_tokens(i, carry):
        token(2 * i, tile_a)
        token(2 * i + 1, tile_b)
        return carry

    lax.fori_loop(0, TOKEN_TILE // 2, two_tokens, 0)


def _peer_v(idx, w_t, table, d):
    n, nsel = idx.shape
    return pl.pallas_call(
        _peer_v_kernel,
        grid=(n // TOKEN_TILE,),
        in_specs=[pl.BlockSpec((TOKEN_TILE, nsel), lambda i: (i, 0), memory_space=pltpu.SMEM),
                  pl.BlockSpec((nsel, TOKEN_TILE), lambda i: (0, i)),
                  _resident_table_spec(table)],
        out_specs=pl.BlockSpec((TOKEN_TILE, d // LANES, LANES), lambda i: (i, 0, 0)),
        out_shape=jax.ShapeDtypeStruct((n, d // LANES, LANES), _f32),
        scratch_shapes=[pltpu.VMEM((SLAB * SLAB_STRIDE, LANES), jnp.uint32),
                        pltpu.VMEM((SLAB * SLAB_STRIDE, LANES), jnp.uint32)],
        compiler_params=pltpu.CompilerParams(
            dimension_semantics=("arbitrary",), vmem_limit_bytes=VMEM_LIMIT),
        name="peer_v_pass",
    )(idx, w_t, table)


def _final_kernel(x1_ref, p_ref, g_ref, y_ref):
    y_ref[...] = _rms(x1_ref[...] + p_ref[...], g_ref[...])


def _final(x1, peer, g):
    n, d = x1.shape
    tm = _row_tile(n)
    row = lambda i: (i, 0)
    return pl.pallas_call(
        _final_kernel,
        grid=(n // tm,),
        in_specs=[pl.BlockSpec((tm, d), row), pl.BlockSpec((tm, d), row), pl.BlockSpec((1, d), lambda i: (0, 0))],
        out_specs=pl.BlockSpec((tm, d), row),
        out_shape=jax.ShapeDtypeStruct((n, d), _f32),
        compiler_params=pltpu.CompilerParams(
            dimension_semantics=("parallel",), vmem_limit_bytes=VMEM_LIMIT),
        name="residual_final_norm",
    )(x1, peer, g)


def _rel_bias(rel_table, q_pos, k_pos):
    idx = jnp.clip(q_pos[:, None] - k_pos[None, :], -MAX_REL, MAX_REL) + MAX_REL
    return rel_table[:, idx].astype(_f32)


def _block_diag_groups(w):
    nb, bi, bo = w.shape
    per = MXU_DIM // bi
    w = w.reshape(nb // per, per, bi, bo)
    eye = jnp.eye(per, dtype=w.dtype)
    return jnp.einsum("gpio,pq->gpiqo", w, eye).reshape(nb // per, per * bi, per * bo)


def _layer(xp, xs, cache_k, cache_v, state_conv, state_lru, norm_mix, w_in, rel_table, conv_w, conv_b,
           lru_wr, lru_br, lru_wi, lru_bi, lru_lambda, w_ba, w_bl, w_o, norm_ffn, peer_wq, keys1, keys2,
           tab_u, tab_v):
    bsz, seq, d = xp.shape
    dbsz, dseq, _ = xs.shape
    n_p, n_s = bsz * seq, dbsz * dseq
    x_all = jnp.concatenate([xp.reshape(n_p, d), xs.reshape(n_s, d)], axis=0)
    row = lambda v: v.reshape(1, -1)

    proj = _inproj(x_all, row(norm_mix), w_in.astype(_bf16))

    tiles = seq // BAND
    bias_p = _rel_bias(rel_table, jnp.arange(CHUNK) + BAND, jnp.arange(BAND + CHUNK))
    prev = lambda c: (lambda b, i: (b * tiles + jnp.maximum(i - 1, 0), c))
    cur = lambda c: (lambda b, i: (b * tiles + i, c))
    o_attn_p = _attention(
        proj, proj, proj, bias_p, out_rows=n_p, n_chunks=BAND // CHUNK, cq=CHUNK, mask_first=True,
        grid=(bsz, tiles), q_map=cur(0), k_map=cur(1), v_map=cur(2),
        kprev_spec=pl.BlockSpec((BAND, d), prev(1)), vprev_spec=pl.BlockSpec((BAND, d), prev(2)))
    rows_c = cache_k.shape[1]
    bias_s = _rel_bias(rel_table, PAST_LEN + jnp.arange(dseq), PAST_LEN - rows_c + jnp.arange(rows_c + dseq))
    blk_s = n_p // dseq
    samp = lambda c: (lambda b, i: (blk_s + b, c))
    cache_spec = pl.BlockSpec((None, rows_c, d), lambda b, i: (b, 0, 0))
    o_attn_s = _attention(
        proj, cache_k.reshape(dbsz, rows_c, d), cache_v.reshape(dbsz, rows_c, d), bias_s,
        out_rows=n_s, n_chunks=1, cq=dseq, mask_first=False, grid=(dbsz, 1),
        q_map=samp(0), k_map=samp(1), v_map=samp(2), kprev_spec=cache_spec, vprev_spec=cache_spec)

    lru_w = (conv_w, row(conv_b), _block_diag_groups(lru_wr).astype(_bf16), row(lru_br),
             _block_diag_groups(lru_wi).astype(_bf16), row(lru_bi), row(lru_lambda))
    pad_state = lambda st: jnp.pad(st, ((0, 0), (SUBLANES - (CONV_W - 1), 0), (0, 0)))
    o_lru_p, h_p = _lru(proj, jnp.zeros((bsz, SUBLANES, d), _f32), jnp.zeros((bsz, 1, d), _f32), lru_w,
                        n_batch=bsz, seq=seq, tt=256, row0=0)
    o_lru_s, h_s = _lru(proj, pad_state(state_conv), state_lru.reshape(dbsz, 1, d), lru_w,
                        n_batch=dbsz, seq=dseq, tt=dseq, row0=n_p)

    o_attn = jnp.concatenate([o_attn_p, o_attn_s], axis=0)
    o_lru = jnp.concatenate([o_lru_p, o_lru_s], axis=0)
    x1, xt, qp = _merge(x_all, o_attn, o_lru, proj, w_ba.astype(_bf16), w_bl.astype(_bf16),
                        w_o.astype(_bf16), row(norm_ffn), peer_wq.astype(_bf16))

    idx_t, g_t = _topk(qp, keys1.astype(_bf16), keys2.astype(_bf16))
    idx = idx_t.T
    w_t = _peer_u(idx, xt, g_t, tab_u)
    peer = _peer_v(idx, w_t, tab_v, d).reshape(n_p + n_s, d)

    k_p = proj[:n_p, d:2 * d].reshape(bsz, seq, N_HEADS, HEAD_DIM)
    v_p = proj[:n_p, 2 * d:3 * d].reshape(bsz, seq, N_HEADS, HEAD_DIM)
    xr_p = proj[:n_p, 3 * d:4 * d].reshape(bsz, seq, d)
    k_s = proj[n_p:, d:2 * d].reshape(dbsz, dseq, N_HEADS, HEAD_DIM)
    v_s = proj[n_p:, 2 * d:3 * d].reshape(dbsz, dseq, N_HEADS, HEAD_DIM)
    xr_s = jnp.concatenate([state_conv, proj[n_p:, 3 * d:4 * d].reshape(dbsz, dseq, d)], axis=1)
    rows_p = min(BAND, seq)
    keep = CONV_W - 1
    return (x1, peer, k_p[:, seq - rows_p:], v_p[:, seq - rows_p:], xr_p[:, seq - keep:], h_p.reshape(bsz, d),
            k_s, v_s, xr_s[:, xr_s.shape[1] - keep:], h_s.reshape(dbsz, d))


def kernel(x_prompt, x_sample, cache_k, cache_v, state_conv, state_lru, norm_mix, w_in, rel_table, conv_w, conv_b, lru_wr, lru_br, lru_wi, lru_bi, lru_lambda, w_branch_attn, w_branch_lru, w_out, norm_ffn, peer_wq, peer_keys1, peer_keys2, peer_u, peer_v, norm_final):
    depth = w_in.shape[0]
    assert depth == 1, "the stacked-token pipeline is written for a single layer"
    bsz, seq, d = x_prompt.shape
    dbsz, dseq, _ = x_sample.shape
    n_p = bsz * seq
    l = 0
    (x1, peer, k_p, v_p, c_p, h_p, k_s, v_s, c_s, h_s) = _layer(
        x_prompt, x_sample, cache_k[l], cache_v[l], state_conv[l], state_lru[l], norm_mix[l], w_in[l],
        rel_table[l], conv_w[l], conv_b[l], lru_wr[l], lru_br[l], lru_wi[l], lru_bi[l], lru_lambda[l],
        w_branch_attn[l], w_branch_lru[l], w_out[l], norm_ffn[l], peer_wq[l], peer_keys1[l], peer_keys2[l],
        _pack_table(peer_u[l]), _pack_table(peer_v[l]))
    y = _final(x1, peer, norm_final.reshape(1, -1))
    y_prompt = y[:n_p].reshape(bsz, seq, d)
    y_sample = y[n_p:].reshape(dbsz, dseq, d)
    stack = lambda v: v[None]
    return (y_prompt, y_sample, stack(k_p), stack(v_p), stack(c_p), stack(h_p),
            stack(k_s), stack(v_s), stack(c_s), stack(h_s))
```

```python
import functools

import jax
import jax.numpy as jnp
from jax import lax
from jax.experimental import pallas as pl
from jax.experimental.pallas import tpu as pltpu

CHUNK = 64
LEFT_CHUNKS = 8
BAND = LEFT_CHUNKS * CHUNK
N_HEADS = 16
HEAD_DIM = 64
ATTN_SCALE = HEAD_DIM ** -0.5
MAX_REL = 128
PAST_LEN = 2048
LRU_BLOCKS = 16
CONV_W = 4
LRU_C = 8.0
PEER_HEADS = 8
N_KEYS = 128
TOPK = 16
EPS = 1e-6
NEG_INF = -1e30

LANES = 128
SUBLANES = 8
MXU_DIM = 256
TOKEN_TILE = 128
SLAB = 4
SLAB_STRIDE = 136
VMEM_LIMIT = 56 * 1024 * 1024

_bf16 = jnp.bfloat16
_f32 = jnp.float32


def _row_tile(n_rows):
    for t in (512, 384, 256, 128):
        if n_rows % t == 0:
            return t
    raise ValueError(f"token count {n_rows} must be a multiple of 128")


def _rms(x, g):
    return x * lax.rsqrt(jnp.mean(x * x, axis=-1, keepdims=True) + EPS) * g


def _inproj_kernel(x_ref, g_ref, w_ref, o_ref):
    d = x_ref.shape[1]
    xn = _rms(x_ref[...], g_ref[...]).astype(_bf16)
    for j in range(w_ref.shape[1] // d):
        cols = slice(j * d, (j + 1) * d)
        o_ref[:, cols] = jnp.dot(xn, w_ref[:, cols], preferred_element_type=_f32)


def _inproj(x, g, w):
    n, d = x.shape
    cols = w.shape[1]
    tm = 256 if n % 256 == 0 else TOKEN_TILE
    return pl.pallas_call(
        _inproj_kernel,
        grid=(n // tm,),
        in_specs=[pl.BlockSpec((tm, d), lambda i: (i, 0)),
                  pl.BlockSpec((1, d), lambda i: (0, 0)),
                  pl.BlockSpec((d, cols), lambda i: (0, 0), pipeline_mode=pl.Buffered(1))],
        out_specs=pl.BlockSpec((tm, cols), lambda i: (i, 0)),
        out_shape=jax.ShapeDtypeStruct((n, cols), _f32),
        compiler_params=pltpu.CompilerParams(
            dimension_semantics=("parallel",), vmem_limit_bytes=VMEM_LIMIT),
        name="inproj",
    )(x, g, w)


def _attn_kernel(q_ref, kp_ref, kc_ref, vp_ref, vc_ref, bias_ref, o_ref, kcat, vcat,
                 *, n_chunks, cq, mask_first):
    nq = n_chunks * cq
    span = BAND + cq
    kcat[0:BAND, :] = kp_ref[...].astype(_bf16)
    kcat[BAND:BAND + nq, :] = kc_ref[...].astype(_bf16)
    vcat[0:BAND, :] = vp_ref[...].astype(_bf16)
    vcat[BAND:BAND + nq, :] = vc_ref[...].astype(_bf16)
    first = pl.program_id(1) == 0
    pair = 2 * HEAD_DIM
    low_head = lax.broadcasted_iota(jnp.int32, (cq, pair), 1) < HEAD_DIM

    def chunk(cl, carry):
        r0 = pl.multiple_of(cl * cq, cq)
        q = q_ref[pl.ds(r0, cq), :] * ATTN_SCALE
        kw = kcat[pl.ds(r0, span), :]
        vw = vcat[pl.ds(r0, span), :]
        if mask_first:
            col = lax.broadcasted_iota(jnp.int32, (cq, span), 1)
            valid = jnp.logical_or(jnp.logical_not(first), col + r0 >= BAND)
        scores = []
        for pr in range(N_HEADS // 2):
            sl = slice(pr * pair, (pr + 1) * pair)
            for keep_low in (True, False):
                qh = (jnp.where(low_head, q[:, sl], 0.0) if keep_low
                      else jnp.where(low_head, 0.0, q[:, sl])).astype(_bf16)
                scores.append(lax.dot_general(qh, kw[:, sl], (((1,), (1,)), ((), ())),
                                              preferred_element_type=_f32))
        probs, inv = [], []
        for h in range(N_HEADS):
            s = scores[h] + bias_ref[h]
            if mask_first:
                s = jnp.where(valid, s, NEG_INF)
            p = jnp.exp(s - jnp.max(s, axis=-1, keepdims=True))
            inv.append(1.0 / jnp.sum(p, axis=-1, keepdims=True))
            probs.append(p.astype(_bf16))
        outs = []
        for pr in range(N_HEADS // 2):
            vp = vw[:, pr * pair:(pr + 1) * pair]
            o_lo = jnp.dot(probs[2 * pr], vp, preferred_element_type=_f32) * inv[2 * pr]
            o_hi = jnp.dot(probs[2 * pr + 1], vp, preferred_element_type=_f32) * inv[2 * pr + 1]
            outs.append(jnp.where(low_head, o_lo, o_hi))
        o_ref[pl.ds(r0, cq), :] = jnp.concatenate(outs, axis=-1).astype(o_ref.dtype)
        return carry

    lax.fori_loop(0, n_chunks, chunk, 0)


def _attention(q_src, kprev_src, vprev_src, bias, *, out_rows, n_chunks, cq, mask_first, grid,
               q_map, k_map, v_map, kprev_spec, vprev_spec):
    nq = n_chunks * cq
    d = N_HEADS * HEAD_DIM
    span = BAND + cq
    kern = functools.partial(_attn_kernel, n_chunks=n_chunks, cq=cq, mask_first=mask_first)
    return pl.pallas_call(
        kern,
        grid=grid,
        in_specs=[pl.BlockSpec((nq, d), q_map),
                  kprev_spec,
                  pl.BlockSpec((nq, d), k_map),
                  vprev_spec,
                  pl.BlockSpec((nq, d), v_map),
                  pl.BlockSpec((N_HEADS, cq, span), lambda b, i: (0, 0, 0))],
        out_specs=pl.BlockSpec((nq, d), lambda b, i, g=grid: (b * g[1] + i, 0)),
        out_shape=jax.ShapeDtypeStruct((out_rows, d), _bf16),
        scratch_shapes=[pltpu.VMEM((BAND + nq, d), _bf16), pltpu.VMEM((BAND + nq, d), _bf16)],
        compiler_params=pltpu.CompilerParams(
            dimension_semantics=("parallel", "arbitrary"), vmem_limit_bytes=VMEM_LIMIT),
        name="band_attention",
    )(q_src, kprev_src, q_src, vprev_src, q_src, bias)


def _shift_rows(x, s, fill, row):
    return jnp.where(row >= s, pltpu.roll(x, s, axis=0), fill)


def _lru_kernel(x_ref, gb_ref, cinit_ref, h0_ref, cw_ref, cb_ref, wr_ref, br_ref, wi_ref, bi_ref,
                lam_ref, o_ref, hl_ref, xbuf, hcar, *, tt):
    d = x_ref.shape[-1]

    @pl.when(pl.program_id(1) == 0)
    def _():
        xbuf[0:SUBLANES, :] = cinit_ref[...]
        hcar[...] = h0_ref[...]

    xbuf[SUBLANES:SUBLANES + tt, :] = x_ref[...]
    base = SUBLANES - (CONV_W - 1)
    xc = cb_ref[...]
    for k in range(CONV_W):
        xc = xc + xbuf[pl.ds(base + k, tt), :] * cw_ref[k:k + 1, :]
    xbuf[0:SUBLANES, :] = xbuf[tt:tt + SUBLANES, :]

    xcb = xc.astype(_bf16)
    n_groups = d // MXU_DIM
    r_parts, i_parts = [], []
    for g in range(n_groups):
        xs = xcb[:, g * MXU_DIM:(g + 1) * MXU_DIM]
        r_parts.append(jnp.dot(xs, wr_ref[g], preferred_element_type=_f32))
        i_parts.append(jnp.dot(xs, wi_ref[g], preferred_element_type=_f32))
    r = jax.nn.sigmoid(jnp.concatenate(r_parts, axis=-1) + br_ref[...])
    gi = jax.nn.sigmoid(jnp.concatenate(i_parts, axis=-1) + bi_ref[...])
    z = -lam_ref[...]
    softplus = jnp.maximum(z, 0.0) + jnp.log(1.0 + jnp.exp(-jnp.abs(z)))
    log_a = (-LRU_C) * r * softplus
    a = jnp.exp(log_a)
    b = jnp.sqrt(1.0 - a * a) * (gi * xc)

    row = lax.broadcasted_iota(jnp.int32, (tt, d), 0)
    s = 1
    while s < tt:
        b = a * _shift_rows(b, s, 0.0, row) + b
        a = a * _shift_rows(a, s, 1.0, row)
        s *= 2
    h = a * hcar[...] + b
    h_last = h[tt - 1:tt, :]
    hcar[...] = h_last
    hl_ref[...] = h_last
    o_ref[...] = (h * jax.nn.gelu(gb_ref[...])).astype(o_ref.dtype)


def _lru(proj, conv_init, h0, weights, *, n_batch, seq, tt, row0):
    d = conv_init.shape[-1]
    steps = seq // tt
    blk0 = row0 // tt
    (cw, cb, wr, br, wi, bi, lam) = weights
    const2 = lambda b, i: (0, 0)
    const3 = lambda b, i: (0, 0, 0)
    kern = functools.partial(_lru_kernel, tt=tt)
    return pl.pallas_call(
        kern,
        grid=(n_batch, steps),
        in_specs=[pl.BlockSpec((tt, d), lambda b, i: (blk0 + b * steps + i, 3)),
                  pl.BlockSpec((tt, d), lambda b, i: (blk0 + b * steps + i, 4)),
                  pl.BlockSpec((None, SUBLANES, d), lambda b, i: (b, 0, 0)),
                  pl.BlockSpec((None, 1, d), lambda b, i: (b, 0, 0)),
                  pl.BlockSpec((CONV_W, d), const2),
                  pl.BlockSpec((1, d), const2),
                  pl.BlockSpec(wr.shape, const3),
                  pl.BlockSpec((1, d), const2),
                  pl.BlockSpec(wi.shape, const3),
                  pl.BlockSpec((1, d), const2),
                  pl.BlockSpec((1, d), const2)],
        out_specs=[pl.BlockSpec((tt, d), lambda b, i: (b * steps + i, 0)),
                   pl.BlockSpec((None, 1, d), lambda b, i: (b, 0, 0))],
        out_shape=[jax.ShapeDtypeStruct((n_batch * seq, d), _bf16),
                   jax.ShapeDtypeStruct((n_batch, 1, d), _f32)],
        scratch_shapes=[pltpu.VMEM((tt + 2 * SUBLANES, d), _f32), pltpu.VMEM((1, d), _f32)],
        compiler_params=pltpu.CompilerParams(
            dimension_semantics=("parallel", "arbitrary"), vmem_limit_bytes=VMEM_LIMIT),
        name="conv_rglru",
    )(proj, proj, conv_init, h0, cw, cb, wr, br, wi, bi, lam)


def _merge_kernel(x_ref, oa_ref, ol_ref, ga_ref, gr_ref, wba_ref, wbl_ref, wo_ref, gn_ref, wq_ref,
                  x1_ref, xt_ref, qp_ref):
    ya = jnp.dot(oa_ref[...], wba_ref[...], preferred_element_type=_f32)
    yl = jnp.dot(ol_ref[...], wbl_ref[...], preferred_element_type=_f32)
    merged = jax.nn.sigmoid(ga_ref[...]) * ya + jax.nn.sigmoid(gr_ref[...]) * yl
    x1 = x_ref[...] + jnp.dot(merged.astype(_bf16), wo_ref[...], preferred_element_type=_f32)
    x1_ref[...] = x1
    xt = _rms(x1, gn_ref[...])
    xt_ref[...] = xt
    qp_ref[...] = jnp.dot(xt.astype(_bf16), wq_ref[...], preferred_element_type=_f32).astype(_bf16)


def _merge(x, o_attn, o_lru, proj, w_ba, w_bl, w_o, g_ffn, wq):
    n, d = x.shape
    tm = _row_tile(n)
    nq = wq.shape[1]
    row = lambda i: (i, 0)
    const = lambda i: (0, 0)
    return pl.pallas_call(
        _merge_kernel,
        grid=(n // tm,),
        in_specs=[pl.BlockSpec((tm, d), row),
                  pl.BlockSpec((tm, d), row),
                  pl.BlockSpec((tm, d), row),
                  pl.BlockSpec((tm, d), lambda i: (i, 5)),
                  pl.BlockSpec((tm, d), lambda i: (i, 6)),
                  pl.BlockSpec((d, d), const, pipeline_mode=pl.Buffered(1)),
                  pl.BlockSpec((d, d), const, pipeline_mode=pl.Buffered(1)),
                  pl.BlockSpec((d, d), const, pipeline_mode=pl.Buffered(1)),
                  pl.BlockSpec((1, d), const),
                  pl.BlockSpec((d, nq), const, pipeline_mode=pl.Buffered(1))],
        out_specs=[pl.BlockSpec((tm, d), row), pl.BlockSpec((tm, d), row), pl.BlockSpec((tm, nq), row)],
        out_shape=[jax.ShapeDtypeStruct((n, d), _f32), jax.ShapeDtypeStruct((n, d), _f32),
                   jax.ShapeDtypeStruct((n, nq), _bf16)],
        compiler_params=pltpu.CompilerParams(
            dimension_semantics=("parallel",), vmem_limit_bytes=VMEM_LIMIT),
        name="merge_outproj_peerq",
    )(x, o_attn, o_lru, proj, proj, w_ba, w_bl, w_o, g_ffn, wq)


def _top16(s, payload=None):
    n_rows = s.shape[0]
    row = lax.broadcasted_iota(jnp.int32, s.shape, 0).astype(_f32)
    vals, picks = [], []
    for _ in range(TOPK):
        m = jnp.max(s, axis=0, keepdims=True)
        am = jnp.min(jnp.where(s == m, row, float(n_rows)), axis=0, keepdims=True)
        hit = row == am
        vals.append(m)
        if payload is None:
            picks.append(am)
        else:
            picks.append(jnp.max(jnp.where(hit, payload, -1.0), axis=0, keepdims=True))
        s = jnp.where(hit, -jnp.inf, s)
    return jnp.concatenate(vals, axis=0), jnp.concatenate(picks, axis=0)


def _candidates(a, b, mask_out):
    j8 = lax.broadcasted_iota(jnp.int32, (SUBLANES, a.shape[1]), 0)
    pieces = [a[0:1, :] + b]
    for i in range(1, SUBLANES):
        piece = a[i:i + 1, :] + b[0:SUBLANES, :]
        if mask_out:
            piece = jnp.where(j8 < TOPK // (i + 1), piece, -jnp.inf)
        pieces.append(piece)
    pieces.append(a[SUBLANES:TOPK, :] + b[0:1, :])
    return jnp.concatenate(pieces, axis=0)


def _topk_kernel(qp_ref, k1_ref, k2_ref, idx_ref, g_ref):
    nt = (((1,), (1,)), ((), ()))
    picks = []
    for h in range(PEER_HEADS):
        c0 = h * 2 * N_KEYS
        q1 = qp_ref[:, c0:c0 + N_KEYS]
        q2 = qp_ref[:, c0 + N_KEYS:c0 + 2 * N_KEYS]
        s1 = lax.dot_general(k1_ref[h], q1, nt, preferred_element_type=_f32)
        s2 = lax.dot_general(k2_ref[h], q2, nt, preferred_element_type=_f32)
        v1, i1 = _top16(s1)
        v2, i2 = _top16(s2)
        cand_s = _candidates(v1, v2, True)
        cand_i = _candidates(i1 * float(N_KEYS), i2, False)
        top_s, e_idx = _top16(cand_s, cand_i)
        p = jnp.exp(top_s - jnp.max(top_s, axis=0, keepdims=True))
        g = p / jnp.sum(p, axis=0, keepdims=True)
        picks.append(e_idx)
        g_ref[h * TOPK:(h + 1) * TOPK, :] = g
    idx_ref[...] = jnp.concatenate(picks, axis=0).T.astype(jnp.int32) * SLAB


def _topk(qp, keys1, keys2):
    n = qp.shape[0]
    nsel = PEER_HEADS * TOPK
    const3 = lambda i: (0, 0, 0)
    return pl.pallas_call(
        _topk_kernel,
        grid=(n // TOKEN_TILE,),
        in_specs=[pl.BlockSpec((TOKEN_TILE, qp.shape[1]), lambda i: (i, 0)),
                  pl.BlockSpec(keys1.shape, const3),
                  pl.BlockSpec(keys2.shape, const3)],
        out_specs=[pl.BlockSpec((TOKEN_TILE, nsel), lambda i: (i, 0)),
                   pl.BlockSpec((nsel, TOKEN_TILE), lambda i: (0, i))],
        out_shape=[jax.ShapeDtypeStruct((n, nsel), jnp.int32), jax.ShapeDtypeStruct((nsel, n), _f32)],
        compiler_params=pltpu.CompilerParams(
            dimension_semantics=("parallel",), vmem_limit_bytes=VMEM_LIMIT),
        name="peer_topk",
    )(qp, keys1, keys2)


def _pack_table(t):
    e, d = t.shape
    bits = lax.bitcast_convert_type(t.astype(_bf16), jnp.uint16).astype(jnp.uint32)
    words = bits[:, :d // 2] | (bits[:, d // 2:] << 16)
    return words.reshape(e * SLAB, LANES)


def _unpack(words):
    lo = pltpu.bitcast(words << 16, _f32)
    hi = pltpu.bitcast(words & jnp.uint32(0xFFFF0000), _f32)
    return lo, hi


def _gather_transposed(idx_ref, t, tab_ref, tile):
    for j in range(idx_ref.shape[1]):
        r = pl.multiple_of(idx_ref[t, j], SLAB)
        tile[pl.ds(j, SLAB, stride=SLAB_STRIDE), :] = tab_ref[pl.ds(r, SLAB), :]


def _tile_chunk(tile, s, nsel):
    return _unpack(tile[s * SLAB_STRIDE:s * SLAB_STRIDE + nsel, :])


def _split_bf16(x):
    hi = x.astype(_bf16)
    return hi, (x - hi.astype(_f32)).astype(_bf16)


def _mxu_right(x, rhs):
    hi, lo = _split_bf16(x)
    return (jnp.dot(hi, rhs, preferred_element_type=_f32) + jnp.dot(lo, rhs, preferred_element_type=_f32))


def _peer_u_kernel(idx_ref, xt_ref, g_ref, tab_ref, w_ref, tile_a, tile_b, part_a, part_b, act_ref):
    nsel = idx_ref.shape[1]
    lane = lax.broadcasted_iota(jnp.int32, (nsel, TOKEN_TILE), 1)
    ones = jnp.ones((LANES, LANES), _bf16)
    for ref in (part_a, part_b, act_ref):
        ref[...] = jnp.zeros_like(ref)

    def partial_dots(t, tile):
        x = xt_ref[t]
        acc = jnp.zeros((nsel, LANES), _f32)
        for s in range(SLAB):
            lo, hi = _tile_chunk(tile, s, nsel)
            acc = acc + lo * x[s:s + 1, :] + hi * x[SLAB + s:SLAB + s + 1, :]
        return acc

    def commit(t):
        dots_a = _mxu_right(part_a[...], ones)
        dots_b = _mxu_right(part_b[...], ones)
        return lambda act: jnp.where(lane == t - 1, dots_b, jnp.where(lane == t - 2, dots_a, act))

    def two_tokens(i, carry):
        t = 2 * i
        place = commit(t)
        _gather_transposed(idx_ref, t, tab_ref, tile_a)
        _gather_transposed(idx_ref, t + 1, tab_ref, tile_b)
        part_a[...] = partial_dots(t, tile_a)
        part_b[...] = partial_dots(t + 1, tile_b)
        act_ref[...] = place(act_ref[...])
        return carry

    lax.fori_loop(0, TOKEN_TILE // 2, two_tokens, 0)
    act = commit(TOKEN_TILE)(act_ref[...])
    w_ref[...] = g_ref[...] * jax.nn.gelu(act)


def _peer_u(idx, xt, g_t, table):
    n, d = xt.shape
    nsel = idx.shape[1]
    xt = xt.reshape(n, d // LANES, LANES)
    return pl.pallas_call(
        _peer_u_kernel,
        grid=(n // TOKEN_TILE,),
        in_specs=[pl.BlockSpec((TOKEN_TILE, nsel), lambda i: (i, 0), memory_space=pltpu.SMEM),
                  pl.BlockSpec((TOKEN_TILE, d // LANES, LANES), lambda i: (i, 0, 0)),
                  pl.BlockSpec((nsel, TOKEN_TILE), lambda i: (0, i)),
                  _resident_table_spec(table)],
        out_specs=pl.BlockSpec((nsel, TOKEN_TILE), lambda i: (0, i)),
        out_shape=jax.ShapeDtypeStruct((nsel, n), _f32),
        scratch_shapes=[pltpu.VMEM((SLAB * SLAB_STRIDE, LANES), jnp.uint32),
                        pltpu.VMEM((SLAB * SLAB_STRIDE, LANES), jnp.uint32),
                        pltpu.VMEM((nsel, LANES), _f32),
                        pltpu.VMEM((nsel, LANES), _f32),
                        pltpu.VMEM((nsel, TOKEN_TILE), _f32)],
        compiler_params=pltpu.CompilerParams(
            dimension_semantics=("arbitrary",), vmem_limit_bytes=VMEM_LIMIT),
        name="peer_u_pass",
    )(idx, xt, g_t, table)


def _resident_table_spec(table):
    return pl.BlockSpec(table.shape, lambda i: (0, 0), pipeline_mode=pl.Buffered(1))


def _peer_v_kernel(idx_ref, w_ref, tab_ref, o_ref, tile_a, tile_b):
    nsel = idx_ref.shape[1]
    lane = lax.broadcasted_iota(jnp.int32, (nsel, TOKEN_TILE), 1)

    def token(t, tile):
        _gather_transposed(idx_ref, t, tab_ref, tile)
        w_col = jnp.sum(jnp.where(lane == t, w_ref[...], 0.0), axis=1, keepdims=True)
        rows_lo, rows_hi = [], []
        for s in range(SLAB):
            lo, hi = _tile_chunk(tile, s, nsel)
            rows_lo.append(jnp.sum(lo * w_col, axis=0, keepdims=True))
            rows_hi.append(jnp.sum(hi * w_col, axis=0, keepdims=True))
        o_ref[t] = jnp.concatenate(rows_lo + rows_hi, axis=0)

    def two_tokens(i, carry):
        token(2 * i, tile_a)
        token(2 * i + 1, tile_b)
        return carry

    lax.fori_loop(0, TOKEN_TILE // 2, two_tokens, 0)


def _peer_v(idx, w_t, table, d):
    n, nsel = idx.shape
    return pl.pallas_call(
        _peer_v_kernel,
        grid=(n // TOKEN_TILE,),
        in_specs=[pl.BlockSpec((TOKEN_TILE, nsel), lambda i: (i, 0), memory_space=pltpu.SMEM),
                  pl.BlockSpec((nsel, TOKEN_TILE), lambda i: (0, i)),
                  _resident_table_spec(table)],
        out_specs=pl.BlockSpec((TOKEN_TILE, d // LANES, LANES), lambda i: (i, 0, 0)),
        out_shape=jax.ShapeDtypeStruct((n, d // LANES, LANES), _f32),
        scratch_shapes=[pltpu.VMEM((SLAB * SLAB_STRIDE, LANES), jnp.uint32),
                        pltpu.VMEM((SLAB * SLAB_STRIDE, LANES), jnp.uint32)],
        compiler_params=pltpu.CompilerParams(
            dimension_semantics=("arbitrary",), vmem_limit_bytes=VMEM_LIMIT),
        name="peer_v_pass",
    )(idx, w_t, table)


def _final_kernel(x1_ref, p_ref, g_ref, y_ref):
    y_ref[...] = _rms(x1_ref[...] + p_ref[...], g_ref[...])


def _final(x1, peer, g):
    n, d = x1.shape
    tm = _row_tile(n)
    row = lambda i: (i, 0)
    return pl.pallas_call(
        _final_kernel,
        grid=(n // tm,),
        in_specs=[pl.BlockSpec((tm, d), row), pl.BlockSpec((tm, d), row), pl.BlockSpec((1, d), lambda i: (0, 0))],
        out_specs=pl.BlockSpec((tm, d), row),
        out_shape=jax.ShapeDtypeStruct((n, d), _f32),
        compiler_params=pltpu.CompilerParams(
            dimension_semantics=("parallel",), vmem_limit_bytes=VMEM_LIMIT),
        name="residual_final_norm",
    )(x1, peer, g)


def _rel_bias(rel_table, q_pos, k_pos):
    idx = jnp.clip(q_pos[:, None] - k_pos[None, :], -MAX_REL, MAX_REL) + MAX_REL
    return rel_table[:, idx].astype(_f32)


def _block_diag_groups(w):
    nb, bi, bo = w.shape
    per = MXU_DIM // bi
    w = w.reshape(nb // per, per, bi, bo)
    eye = jnp.eye(per, dtype=w.dtype)
    return jnp.einsum("gpio,pq->gpiqo", w, eye).reshape(nb // per, per * bi, per * bo)


def _layer(xp, xs, cache_k, cache_v, state_conv, state_lru, norm_mix, w_in, rel_table, conv_w, conv_b,
           lru_wr, lru_br, lru_wi, lru_bi, lru_lambda, w_ba, w_bl, w_o, norm_ffn, peer_wq, keys1, keys2,
           tab_u, tab_v, norm_final):
    bsz, seq, d = xp.shape
    dbsz, dseq, _ = xs.shape
    n_p, n_s = bsz * seq, dbsz * dseq
    xp2, xs2 = xp.reshape(n_p, d), xs.reshape(n_s, d)
    row = lambda v: v.reshape(1, -1)

    w_in_b = w_in.astype(_bf16)
    proj = _inproj(xp2, row(norm_mix), w_in_b)
    proj_s = _inproj(xs2, row(norm_mix), w_in_b)

    tiles = seq // BAND
    bias_p = _rel_bias(rel_table, jnp.arange(CHUNK) + BAND, jnp.arange(BAND + CHUNK))
    prev = lambda c: (lambda b, i: (b * tiles + jnp.maximum(i - 1, 0), c))
    cur = lambda c: (lambda b, i: (b * tiles + i, c))
    o_attn_p = _attention(
        proj, proj, proj, bias_p, out_rows=n_p, n_chunks=BAND // CHUNK, cq=CHUNK, mask_first=True,
        grid=(bsz, tiles), q_map=cur(0), k_map=cur(1), v_map=cur(2),
        kprev_spec=pl.BlockSpec((BAND, d), prev(1)), vprev_spec=pl.BlockSpec((BAND, d), prev(2)))
    rows_c = cache_k.shape[1]
    bias_s = _rel_bias(rel_table, PAST_LEN + jnp.arange(dseq), PAST_LEN - rows_c + jnp.arange(rows_c + dseq))
    samp = lambda c: (lambda b, i: (b, c))
    cache_spec = pl.BlockSpec((None, rows_c, d), lambda b, i: (b, 0, 0))
    o_attn_s = _attention(
        proj_s, cache_k.reshape(dbsz, rows_c, d), cache_v.reshape(dbsz, rows_c, d), bias_s,
        out_rows=n_s, n_chunks=1, cq=dseq, mask_first=False, grid=(dbsz, 1),
        q_map=samp(0), k_map=samp(1), v_map=samp(2), kprev_spec=cache_spec, vprev_spec=cache_spec)

    lru_w = (conv_w, row(conv_b), _block_diag_groups(lru_wr).astype(_bf16), row(lru_br),
             _block_diag_groups(lru_wi).astype(_bf16), row(lru_bi), row(lru_lambda))
    pad_state = lambda st: jnp.pad(st, ((0, 0), (SUBLANES - (CONV_W - 1), 0), (0, 0)))
    o_lru_p, h_p = _lru(proj, jnp.zeros((bsz, SUBLANES, d), _f32), jnp.zeros((bsz, 1, d), _f32), lru_w,
                        n_batch=bsz, seq=seq, tt=256, row0=0)
    o_lru_s, h_s = _lru(proj_s, pad_state(state_conv), state_lru.reshape(dbsz, 1, d), lru_w,
                        n_batch=dbsz, seq=dseq, tt=dseq, row0=0)

    mix_w = (w_ba.astype(_bf16), w_bl.astype(_bf16), w_o.astype(_bf16), row(norm_ffn), peer_wq.astype(_bf16))
    k1, k2 = keys1.astype(_bf16), keys2.astype(_bf16)

    def ffn(x2, o_attn, o_lru, pr):
        x1, xt, qp = _merge(x2, o_attn, o_lru, pr, *mix_w)
        idx, g_t = _topk(qp, k1, k2)
        w_t = _peer_u(idx, xt, g_t, tab_u)
        peer = _peer_v(idx, w_t, tab_v, d).reshape(x2.shape)
        return _final(x1, peer, row(norm_final))

    y_p = ffn(xp2, o_attn_p, o_lru_p, proj).reshape(bsz, seq, d)
    y_s = ffn(xs2, o_attn_s, o_lru_s, proj_s).reshape(dbsz, dseq, d)

    heads = lambda pr, c, b, t: pr[:, c * d:(c + 1) * d].reshape(b, t, N_HEADS, HEAD_DIM)
    xr_p = proj[:, 3 * d:4 * d].reshape(bsz, seq, d)
    xr_s = jnp.concatenate([state_conv, proj_s[:, 3 * d:4 * d].reshape(dbsz, dseq, d)], axis=1)
    rows_p = min(BAND, seq)
    keep = CONV_W - 1
    return (y_p, y_s, heads(proj, 1, bsz, seq)[:, seq - rows_p:], heads(proj, 2, bsz, seq)[:, seq - rows_p:],
            xr_p[:, seq - keep:], h_p.reshape(bsz, d),
            heads(proj_s, 1, dbsz, dseq), heads(proj_s, 2, dbsz, dseq),
            xr_s[:, xr_s.shape[1] - keep:], h_s.reshape(dbsz, d))


def kernel(x_prompt, x_sample, cache_k, cache_v, state_conv, state_lru, norm_mix, w_in, rel_table, conv_w, conv_b, lru_wr, lru_br, lru_wi, lru_bi, lru_lambda, w_branch_attn, w_branch_lru, w_out, norm_ffn, peer_wq, peer_keys1, peer_keys2, peer_u, peer_v, norm_final):
    depth = w_in.shape[0]
    assert depth == 1, "the final norm is fused into the single layer's last kernel"
    l = 0
    (y_prompt, y_sample, k_p, v_p, c_p, h_p, k_s, v_s, c_s, h_s) = _layer(
        x_prompt, x_sample, cache_k[l], cache_v[l], state_conv[l], state_lru[l], norm_mix[l], w_in[l],
        rel_table[l], conv_w[l], conv_b[l], lru_wr[l], lru_br[l], lru_wi[l], lru_bi[l], lru_lambda[l],
        w_branch_attn[l], w_branch_lru[l], w_out[l], norm_ffn[l], peer_wq[l], peer_keys1[l], peer_keys2[l],
        _pack_table(peer_u[l]), _pack_table(peer_v[l]), norm_final)
    stack = lambda v: v[None]
    return (y_prompt, y_sample, stack(k_p), stack(v_p), stack(c_p), stack(h_p),
            stack(k_s), stack(v_s), stack(c_s), stack(h_s))
```

```python
import functools

import jax
import jax.numpy as jnp
from jax import lax
from jax.experimental import pallas as pl
from jax.experimental.pallas import tpu as pltpu

CHUNK = 64
LEFT_CHUNKS = 8
BAND = LEFT_CHUNKS * CHUNK
N_HEADS = 16
HEAD_DIM = 64
ATTN_SCALE = HEAD_DIM ** -0.5
MAX_REL = 128
PAST_LEN = 2048
LRU_BLOCKS = 16
CONV_W = 4
LRU_C = 8.0
PEER_HEADS = 8
N_KEYS = 128
TOPK = 16
EPS = 1e-6
NEG_INF = -1e30

LANES = 128
SUBLANES = 8
MXU_DIM = 256
TOKEN_TILE = 128
SLAB = 4
SLAB_STRIDE = 136
VMEM_LIMIT = 56 * 1024 * 1024

_bf16 = jnp.bfloat16
_f32 = jnp.float32


def _row_tile(n_rows):
    for t in (512, 384, 256, 128):
        if n_rows % t == 0:
            return t
    raise ValueError(f"token count {n_rows} must be a multiple of 128")


def _rms(x, g):
    return x * lax.rsqrt(jnp.mean(x * x, axis=-1, keepdims=True) + EPS) * g


def _inproj_kernel(x_ref, g_ref, w_ref, o_ref):
    d = x_ref.shape[1]
    xn = _rms(x_ref[...], g_ref[...]).astype(_bf16)
    for j in range(w_ref.shape[1] // d):
        cols = slice(j * d, (j + 1) * d)
        o_ref[:, cols] = jnp.dot(xn, w_ref[:, cols], preferred_element_type=_f32)


def _inproj(x, g, w):
    n, d = x.shape
    cols = w.shape[1]
    tm = 256 if n % 256 == 0 else TOKEN_TILE
    return pl.pallas_call(
        _inproj_kernel,
        grid=(n // tm,),
        in_specs=[pl.BlockSpec((tm, d), lambda i: (i, 0)),
                  pl.BlockSpec((1, d), lambda i: (0, 0)),
                  pl.BlockSpec((d, cols), lambda i: (0, 0), pipeline_mode=pl.Buffered(1))],
        out_specs=pl.BlockSpec((tm, cols), lambda i: (i, 0)),
        out_shape=jax.ShapeDtypeStruct((n, cols), _f32),
        compiler_params=pltpu.CompilerParams(
            dimension_semantics=("parallel",), vmem_limit_bytes=VMEM_LIMIT),
        name="inproj",
    )(x, g, w)


def _attn_kernel(q_ref, kp_ref, kc_ref, vp_ref, vc_ref, bias_ref, o_ref, kcat, vcat,
                 *, n_chunks, cq, mask_first):
    nq = n_chunks * cq
    span = BAND + cq
    kcat[0:BAND, :] = kp_ref[...].astype(_bf16)
    kcat[BAND:BAND + nq, :] = kc_ref[...].astype(_bf16)
    vcat[0:BAND, :] = vp_ref[...].astype(_bf16)
    vcat[BAND:BAND + nq, :] = vc_ref[...].astype(_bf16)
    first = pl.program_id(1) == 0
    pair = 2 * HEAD_DIM
    low_head = lax.broadcasted_iota(jnp.int32, (cq, pair), 1) < HEAD_DIM

    def chunk(cl, carry):
        r0 = pl.multiple_of(cl * cq, cq)
        q = q_ref[pl.ds(r0, cq), :] * ATTN_SCALE
        kw = kcat[pl.ds(r0, span), :]
        vw = vcat[pl.ds(r0, span), :]
        if mask_first:
            col = lax.broadcasted_iota(jnp.int32, (cq, span), 1)
            valid = jnp.logical_or(jnp.logical_not(first), col + r0 >= BAND)
        scores = []
        for pr in range(N_HEADS // 2):
            sl = slice(pr * pair, (pr + 1) * pair)
            for keep_low in (True, False):
                qh = (jnp.where(low_head, q[:, sl], 0.0) if keep_low
                      else jnp.where(low_head, 0.0, q[:, sl])).astype(_bf16)
                scores.append(lax.dot_general(qh, kw[:, sl], (((1,), (1,)), ((), ())),
                                              preferred_element_type=_f32))
        probs, inv = [], []
        for h in range(N_HEADS):
            s = scores[h] + bias_ref[h]
            if mask_first:
                s = jnp.where(valid, s, NEG_INF)
            p = jnp.exp(s - jnp.max(s, axis=-1, keepdims=True))
            inv.append(1.0 / jnp.sum(p, axis=-1, keepdims=True))
            probs.append(p.astype(_bf16))
        outs = []
        for pr in range(N_HEADS // 2):
            vp = vw[:, pr * pair:(pr + 1) * pair]
            o_lo = jnp.dot(probs[2 * pr], vp, preferred_element_type=_f32) * inv[2 * pr]
            o_hi = jnp.dot(probs[2 * pr + 1], vp, preferred_element_type=_f32) * inv[2 * pr + 1]
            outs.append(jnp.where(low_head, o_lo, o_hi))
        o_ref[pl.ds(r0, cq), :] = jnp.concatenate(outs, axis=-1).astype(o_ref.dtype)
        return carry

    lax.fori_loop(0, n_chunks, chunk, 0)


def _attention(q_src, kprev_src, vprev_src, bias, *, out_rows, n_chunks, cq, mask_first, grid,
               q_map, k_map, v_map, kprev_spec, vprev_spec):
    nq = n_chunks * cq
    d = N_HEADS * HEAD_DIM
    span = BAND + cq
    kern = functools.partial(_attn_kernel, n_chunks=n_chunks, cq=cq, mask_first=mask_first)
    return pl.pallas_call(
        kern,
        grid=grid,
        in_specs=[pl.BlockSpec((nq, d), q_map),
                  kprev_spec,
                  pl.BlockSpec((nq, d), k_map),
                  vprev_spec,
                  pl.BlockSpec((nq, d), v_map),
                  pl.BlockSpec((N_HEADS, cq, span), lambda b, i: (0, 0, 0))],
        out_specs=pl.BlockSpec((nq, d), lambda b, i, g=grid: (b * g[1] + i, 0)),
        out_shape=jax.ShapeDtypeStruct((out_rows, d), _bf16),
        scratch_shapes=[pltpu.VMEM((BAND + nq, d), _bf16), pltpu.VMEM((BAND + nq, d), _bf16)],
        compiler_params=pltpu.CompilerParams(
            dimension_semantics=("parallel", "arbitrary"), vmem_limit_bytes=VMEM_LIMIT),
        name="band_attention",
    )(q_src, kprev_src, q_src, vprev_src, q_src, bias)


def _shift_rows(x, s, fill, row):
    return jnp.where(row >= s, pltpu.roll(x, s, axis=0), fill)


def _lru_kernel(x_ref, gb_ref, cinit_ref, h0_ref, cw_ref, cb_ref, wr_ref, br_ref, wi_ref, bi_ref,
                lam_ref, o_ref, hl_ref, xbuf, hcar, *, tt):
    d = x_ref.shape[-1]

    @pl.when(pl.program_id(1) == 0)
    def _():
        xbuf[0:SUBLANES, :] = cinit_ref[...]
        hcar[...] = h0_ref[...]

    xbuf[SUBLANES:SUBLANES + tt, :] = x_ref[...]
    base = SUBLANES - (CONV_W - 1)
    xc = cb_ref[...]
    for k in range(CONV_W):
        xc = xc + xbuf[pl.ds(base + k, tt), :] * cw_ref[k:k + 1, :]
    xbuf[0:SUBLANES, :] = xbuf[tt:tt + SUBLANES, :]

    xcb = xc.astype(_bf16)
    n_groups = d // MXU_DIM
    r_parts, i_parts = [], []
    for g in range(n_groups):
        xs = xcb[:, g * MXU_DIM:(g + 1) * MXU_DIM]
        r_parts.append(jnp.dot(xs, wr_ref[g], preferred_element_type=_f32))
        i_parts.append(jnp.dot(xs, wi_ref[g], preferred_element_type=_f32))
    r = jax.nn.sigmoid(jnp.concatenate(r_parts, axis=-1) + br_ref[...])
    gi = jax.nn.sigmoid(jnp.concatenate(i_parts, axis=-1) + bi_ref[...])
    z = -lam_ref[...]
    softplus = jnp.maximum(z, 0.0) + jnp.log(1.0 + jnp.exp(-jnp.abs(z)))
    log_a = (-LRU_C) * r * softplus
    a = jnp.exp(log_a)
    b = jnp.sqrt(1.0 - a * a) * (gi * xc)

    row = lax.broadcasted_iota(jnp.int32, (tt, d), 0)
    s = 1
    while s < tt:
        b = a * _shift_rows(b, s, 0.0, row) + b
        a = a * _shift_rows(a, s, 1.0, row)
        s *= 2
    h = a * hcar[...] + b
    h_last = h[tt - 1:tt, :]
    hcar[...] = h_last
    hl_ref[...] = h_last
    o_ref[...] = (h * jax.nn.gelu(gb_ref[...])).astype(o_ref.dtype)


def _lru(proj, conv_init, h0, weights, *, n_batch, seq, tt, row0):
    d = conv_init.shape[-1]
    steps = seq // tt
    blk0 = row0 // tt
    (cw, cb, wr, br, wi, bi, lam) = weights
    const2 = lambda b, i: (0, 0)
    const3 = lambda b, i: (0, 0, 0)
    kern = functools.partial(_lru_kernel, tt=tt)
    return pl.pallas_call(
        kern,
        grid=(n_batch, steps),
        in_specs=[pl.BlockSpec((tt, d), lambda b, i: (blk0 + b * steps + i, 3)),
                  pl.BlockSpec((tt, d), lambda b, i: (blk0 + b * steps + i, 4)),
                  pl.BlockSpec((None, SUBLANES, d), lambda b, i: (b, 0, 0)),
                  pl.BlockSpec((None, 1, d), lambda b, i: (b, 0, 0)),
                  pl.BlockSpec((CONV_W, d), const2),
                  pl.BlockSpec((1, d), const2),
                  pl.BlockSpec(wr.shape, const3),
                  pl.BlockSpec((1, d), const2),
                  pl.BlockSpec(wi.shape, const3),
                  pl.BlockSpec((1, d), const2),
                  pl.BlockSpec((1, d), const2)],
        out_specs=[pl.BlockSpec((tt, d), lambda b, i: (b * steps + i, 0)),
                   pl.BlockSpec((None, 1, d), lambda b, i: (b, 0, 0))],
        out_shape=[jax.ShapeDtypeStruct((n_batch * seq, d), _bf16),
                   jax.ShapeDtypeStruct((n_batch, 1, d), _f32)],
        scratch_shapes=[pltpu.VMEM((tt + 2 * SUBLANES, d), _f32), pltpu.VMEM((1, d), _f32)],
        compiler_params=pltpu.CompilerParams(
            dimension_semantics=("parallel", "arbitrary"), vmem_limit_bytes=VMEM_LIMIT),
        name="conv_rglru",
    )(proj, proj, conv_init, h0, cw, cb, wr, br, wi, bi, lam)


def _merge_kernel(x_ref, oa_ref, ol_ref, ga_ref, gr_ref, wba_ref, wbl_ref, wo_ref, gn_ref, wq_ref,
                  x1_ref, xt_ref, qp_ref):
    ya = jnp.dot(oa_ref[...], wba_ref[...], preferred_element_type=_f32)
    yl = jnp.dot(ol_ref[...], wbl_ref[...], preferred_element_type=_f32)
    merged = jax.nn.sigmoid(ga_ref[...]) * ya + jax.nn.sigmoid(gr_ref[...]) * yl
    x1 = x_ref[...] + jnp.dot(merged.astype(_bf16), wo_ref[...], preferred_element_type=_f32)
    x1_ref[...] = x1
    xt = _rms(x1, gn_ref[...])
    xt_ref[...] = xt
    qp_ref[...] = jnp.dot(xt.astype(_bf16), wq_ref[...], preferred_element_type=_f32).astype(_bf16)


def _merge(x, o_attn, o_lru, proj, w_ba, w_bl, w_o, g_ffn, wq):
    n, d = x.shape
    tm = _row_tile(n)
    nq = wq.shape[1]
    row = lambda i: (i, 0)
    const = lambda i: (0, 0)
    return pl.pallas_call(
        _merge_kernel,
        grid=(n // tm,),
        in_specs=[pl.BlockSpec((tm, d), row),
                  pl.BlockSpec((tm, d), row),
                  pl.BlockSpec((tm, d), row),
                  pl.BlockSpec((tm, d), lambda i: (i, 5)),
                  pl.BlockSpec((tm, d), lambda i: (i, 6)),
                  pl.BlockSpec((d, d), const, pipeline_mode=pl.Buffered(1)),
                  pl.BlockSpec((d, d), const, pipeline_mode=pl.Buffered(1)),
                  pl.BlockSpec((d, d), const, pipeline_mode=pl.Buffered(1)),
                  pl.BlockSpec((1, d), const),
                  pl.BlockSpec((d, nq), const, pipeline_mode=pl.Buffered(1))],
        out_specs=[pl.BlockSpec((tm, d), row), pl.BlockSpec((tm, d), row), pl.BlockSpec((tm, nq), row)],
        out_shape=[jax.ShapeDtypeStruct((n, d), _f32), jax.ShapeDtypeStruct((n, d), _f32),
                   jax.ShapeDtypeStruct((n, nq), _bf16)],
        compiler_params=pltpu.CompilerParams(
            dimension_semantics=("parallel",), vmem_limit_bytes=VMEM_LIMIT),
        name="merge_outproj_peerq",
    )(x, o_attn, o_lru, proj, proj, w_ba, w_bl, w_o, g_ffn, wq)


def _top16(s, payload=None):
    n_rows = s.shape[0]
    row = lax.broadcasted_iota(jnp.int32, s.shape, 0).astype(_f32)
    vals, picks = [], []
    for _ in range(TOPK):
        m = jnp.max(s, axis=0, keepdims=True)
        am = jnp.min(jnp.where(s == m, row, float(n_rows)), axis=0, keepdims=True)
        hit = row == am
        vals.append(m)
        if payload is None:
            picks.append(am)
        else:
            picks.append(jnp.max(jnp.where(hit, payload, -1.0), axis=0, keepdims=True))
        s = jnp.where(hit, -jnp.inf, s)
    return jnp.concatenate(vals, axis=0), jnp.concatenate(picks, axis=0)


def _candidates(a, b, mask_out):
    j8 = lax.broadcasted_iota(jnp.int32, (SUBLANES, a.shape[1]), 0)
    pieces = [a[0:1, :] + b]
    for i in range(1, SUBLANES):
        piece = a[i:i + 1, :] + b[0:SUBLANES, :]
        if mask_out:
            piece = jnp.where(j8 < TOPK // (i + 1), piece, -jnp.inf)
        pieces.append(piece)
    pieces.append(a[SUBLANES:TOPK, :] + b[0:1, :])
    return jnp.concatenate(pieces, axis=0)


def _topk_kernel(qp_ref, k1_ref, k2_ref, idx_ref, g_ref):
    nt = (((1,), (1,)), ((), ()))
    picks = []
    for h in range(PEER_HEADS):
        c0 = h * 2 * N_KEYS
        q1 = qp_ref[:, c0:c0 + N_KEYS]
        q2 = qp_ref[:, c0 + N_KEYS:c0 + 2 * N_KEYS]
        s1 = lax.dot_general(k1_ref[h], q1, nt, preferred_element_type=_f32)
        s2 = lax.dot_general(k2_ref[h], q2, nt, preferred_element_type=_f32)
        v1, i1 = _top16(s1)
        v2, i2 = _top16(s2)
        cand_s = _candidates(v1, v2, True)
        cand_i = _candidates(i1 * float(N_KEYS), i2, False)
        top_s, e_idx = _top16(cand_s, cand_i)
        p = jnp.exp(top_s - jnp.max(top_s, axis=0, keepdims=True))
        g = p / jnp.sum(p, axis=0, keepdims=True)
        picks.append(e_idx)
        g_ref[h * TOPK:(h + 1) * TOPK, :] = g
    idx_ref[...] = jnp.concatenate(picks, axis=0).T.astype(jnp.int32) * SLAB


def _topk(qp, keys1, keys2):
    n = qp.shape[0]
    nsel = PEER_HEADS * TOPK
    const3 = lambda i: (0, 0, 0)
    return pl.pallas_call(
        _topk_kernel,
        grid=(n // TOKEN_TILE,),
        in_specs=[pl.BlockSpec((TOKEN_TILE, qp.shape[1]), lambda i: (i, 0)),
                  pl.BlockSpec(keys1.shape, const3),
                  pl.BlockSpec(keys2.shape, const3)],
        out_specs=[pl.BlockSpec((TOKEN_TILE, nsel), lambda i: (i, 0)),
                   pl.BlockSpec((nsel, TOKEN_TILE), lambda i: (0, i))],
        out_shape=[jax.ShapeDtypeStruct((n, nsel), jnp.int32), jax.ShapeDtypeStruct((nsel, n), _f32)],
        compiler_params=pltpu.CompilerParams(
            dimension_semantics=("parallel",), vmem_limit_bytes=VMEM_LIMIT),
        name="peer_topk",
    )(qp, keys1, keys2)


def _pack_table(t):
    e, d = t.shape
    bits = lax.bitcast_convert_type(t.astype(_bf16), jnp.uint16).astype(jnp.uint32)
    words = bits[:, :d // 2] | (bits[:, d // 2:] << 16)
    return words.reshape(e * SLAB, LANES)


def _unpack(words):
    lo = pltpu.bitcast(words << 16, _f32)
    hi = pltpu.bitcast(words & jnp.uint32(0xFFFF0000), _f32)
    return lo, hi


def _gather_transposed(idx_ref, t, tab_ref, tile):
    for j in range(idx_ref.shape[1]):
        r = pl.multiple_of(idx_ref[t, j], SLAB)
        tile[pl.ds(j, SLAB, stride=SLAB_STRIDE), :] = tab_ref[pl.ds(r, SLAB), :]


def _tile_chunk(tile, s, nsel):
    return _unpack(tile[s * SLAB_STRIDE:s * SLAB_STRIDE + nsel, :])


def _split_bf16(x):
    hi = x.astype(_bf16)
    return hi, (x - hi.astype(_f32)).astype(_bf16)


def _mxu_right(x, rhs):
    hi, lo = _split_bf16(x)
    return (jnp.dot(hi, rhs, preferred_element_type=_f32) + jnp.dot(lo, rhs, preferred_element_type=_f32))


REDUCE_GROUP = 8


def _peer_u_kernel(idx_ref, xt_ref, g_ref, tab_ref, w_ref, tile_a, tile_b, parts):
    nsel = idx_ref.shape[1]

    def partial_dots(t, tile):
        x = xt_ref[t]
        acc = jnp.zeros((nsel, LANES), _f32)
        for s in range(SLAB):
            lo, hi = _tile_chunk(tile, s, nsel)
            acc = acc + lo * x[s:s + 1, :] + hi * x[SLAB + s:SLAB + s + 1, :]
        return acc

    tile_b[...] = jnp.zeros_like(tile_b)

    def two_tokens(i, carry):
        t = 2 * i
        t_prev = jnp.maximum(t - 1, 0)
        parts[t_prev] = partial_dots(t_prev, tile_b)
        _gather_transposed(idx_ref, t, tab_ref, tile_a)
        _gather_transposed(idx_ref, t + 1, tab_ref, tile_b)
        parts[t] = partial_dots(t, tile_a)
        return carry

    lax.fori_loop(0, TOKEN_TILE // 2, two_tokens, 0)
    parts[TOKEN_TILE - 1] = partial_dots(TOKEN_TILE - 1, tile_b)

    lane = lax.broadcasted_iota(jnp.int32, (nsel, TOKEN_TILE), 1)
    ones = jnp.ones((LANES, LANES), _bf16)

    def reduce_group(gi, act):
        for k in range(REDUCE_GROUP):
            t = gi * REDUCE_GROUP + k
            act = jnp.where(lane == t, _mxu_right(parts[t], ones), act)
        return act

    act = lax.fori_loop(0, TOKEN_TILE // REDUCE_GROUP, reduce_group, jnp.zeros((nsel, TOKEN_TILE), _f32))
    w_ref[...] = g_ref[...] * jax.nn.gelu(act)


def _peer_u(idx, xt, g_t, table):
    n, d = xt.shape
    nsel = idx.shape[1]
    xt = xt.reshape(n, d // LANES, LANES)
    return pl.pallas_call(
        _peer_u_kernel,
        grid=(n // TOKEN_TILE,),
        in_specs=[pl.BlockSpec((TOKEN_TILE, nsel), lambda i: (i, 0), memory_space=pltpu.SMEM),
                  pl.BlockSpec((TOKEN_TILE, d // LANES, LANES), lambda i: (i, 0, 0)),
                  pl.BlockSpec((nsel, TOKEN_TILE), lambda i: (0, i)),
                  _resident_table_spec(table)],
        out_specs=pl.BlockSpec((nsel, TOKEN_TILE), lambda i: (0, i)),
        out_shape=jax.ShapeDtypeStruct((nsel, n), _f32),
        scratch_shapes=[pltpu.VMEM((SLAB * SLAB_STRIDE, LANES), jnp.uint32),
                        pltpu.VMEM((SLAB * SLAB_STRIDE, LANES), jnp.uint32),
                        pltpu.VMEM((TOKEN_TILE, nsel, LANES), _f32)],
        compiler_params=pltpu.CompilerParams(
            dimension_semantics=("arbitrary",), vmem_limit_bytes=VMEM_LIMIT),
        name="peer_u_pass",
    )(idx, xt, g_t, table)


def _resident_table_spec(table):
    return pl.BlockSpec(table.shape, lambda i: (0, 0), pipeline_mode=pl.Buffered(1))


def _peer_v_kernel(idx_ref, w_ref, tab_ref, o_ref, tile_a, tile_b, wcol_b):
    nsel = idx_ref.shape[1]
    lane = lax.broadcasted_iota(jnp.int32, (nsel, TOKEN_TILE), 1)

    def weight_column(t):
        w_col = jnp.sum(jnp.where(lane == t, w_ref[...], 0.0), axis=1, keepdims=True)
        return jnp.broadcast_to(w_col, (nsel, LANES))

    def weighted_sum(t, tile, w_col):
        rows_lo, rows_hi = [], []
        for s in range(SLAB):
            lo, hi = _tile_chunk(tile, s, nsel)
            rows_lo.append(jnp.sum(lo * w_col, axis=0, keepdims=True))
            rows_hi.append(jnp.sum(hi * w_col, axis=0, keepdims=True))
        o_ref[t] = jnp.concatenate(rows_lo + rows_hi, axis=0)

    tile_b[...] = jnp.zeros_like(tile_b)
    wcol_b[...] = jnp.zeros_like(wcol_b)

    def two_tokens(i, carry):
        t = 2 * i
        w_a = weight_column(t)
        w_b = weight_column(t + 1)
        weighted_sum(jnp.maximum(t - 1, 0), tile_b, wcol_b[...])
        _gather_transposed(idx_ref, t, tab_ref, tile_a)
        _gather_transposed(idx_ref, t + 1, tab_ref, tile_b)
        weighted_sum(t, tile_a, w_a)
        wcol_b[...] = w_b
        return carry

    lax.fori_loop(0, TOKEN_TILE // 2, two_tokens, 0)
    weighted_sum(TOKEN_TILE - 1, tile_b, wcol_b[...])


def _peer_v(idx, w_t, table, d):
    n, nsel = idx.shape
    return pl.pallas_call(
        _peer_v_kernel,
        grid=(n // TOKEN_TILE,),
        in_specs=[pl.BlockSpec((TOKEN_TILE, nsel), lambda i: (i, 0), memory_space=pltpu.SMEM),
                  pl.BlockSpec((nsel, TOKEN_TILE), lambda i: (0, i)),
                  _resident_table_spec(table)],
        out_specs=pl.BlockSpec((TOKEN_TILE, d // LANES, LANES), lambda i: (i, 0, 0)),
        out_shape=jax.ShapeDtypeStruct((n, d // LANES, LANES), _f32),
        scratch_shapes=[pltpu.VMEM((SLAB * SLAB_STRIDE, LANES), jnp.uint32),
                        pltpu.VMEM((SLAB * SLAB_STRIDE, LANES), jnp.uint32),
                        pltpu.VMEM((nsel, LANES), _f32)],
        compiler_params=pltpu.CompilerParams(
            dimension_semantics=("arbitrary",), vmem_limit_bytes=VMEM_LIMIT),
        name="peer_v_pass",
    )(idx, w_t, table)


def _final_kernel(x1_ref, p_ref, g_ref, y_ref):
    y_ref[...] = _rms(x1_ref[...] + p_ref[...], g_ref[...])


def _final(x1, peer, g):
    n, d = x1.shape
    tm = _row_tile(n)
    row = lambda i: (i, 0)
    return pl.pallas_call(
        _final_kernel,
        grid=(n // tm,),
        in_specs=[pl.BlockSpec((tm, d), row), pl.BlockSpec((tm, d), row), pl.BlockSpec((1, d), lambda i: (0, 0))],
        out_specs=pl.BlockSpec((tm, d), row),
        out_shape=jax.ShapeDtypeStruct((n, d), _f32),
        compiler_params=pltpu.CompilerParams(
            dimension_semantics=("parallel",), vmem_limit_bytes=VMEM_LIMIT),
        name="residual_final_norm",
    )(x1, peer, g)


def _rel_bias(rel_table, n_q, n_k, offset):
    m = jnp.arange(n_k + n_q - 1)
    diag = rel_table[:, jnp.clip(offset + n_q - 1 - m, -MAX_REL, MAX_REL) + MAX_REL].astype(_f32)
    return jnp.stack([diag[:, n_q - 1 - q:n_q - 1 - q + n_k] for q in range(n_q)], axis=1)


def _block_diag_groups(w):
    nb, bi, bo = w.shape
    per = MXU_DIM // bi
    w = w.reshape(nb // per, per, bi, bo)
    eye = jnp.eye(per, dtype=w.dtype)
    return jnp.einsum("gpio,pq->gpiqo", w, eye).reshape(nb // per, per * bi, per * bo)


def _layer(xp, xs, cache_k, cache_v, state_conv, state_lru, norm_mix, w_in, rel_table, conv_w, conv_b,
           lru_wr, lru_br, lru_wi, lru_bi, lru_lambda, w_ba, w_bl, w_o, norm_ffn, peer_wq, keys1, keys2,
           tab_u, tab_v, norm_final):
    bsz, seq, d = xp.shape
    dbsz, dseq, _ = xs.shape
    n_p, n_s = bsz * seq, dbsz * dseq
    xp2, xs2 = xp.reshape(n_p, d), xs.reshape(n_s, d)
    row = lambda v: v.reshape(1, -1)

    w_in_b = w_in.astype(_bf16)
    proj = _inproj(xp2, row(norm_mix), w_in_b)
    proj_s = _inproj(xs2, row(norm_mix), w_in_b)

    tiles = seq // BAND
    bias_p = _rel_bias(rel_table, CHUNK, BAND + CHUNK, BAND)
    prev = lambda c: (lambda b, i: (b * tiles + jnp.maximum(i - 1, 0), c))
    cur = lambda c: (lambda b, i: (b * tiles + i, c))
    o_attn_p = _attention(
        proj, proj, proj, bias_p, out_rows=n_p, n_chunks=BAND // CHUNK, cq=CHUNK, mask_first=True,
        grid=(bsz, tiles), q_map=cur(0), k_map=cur(1), v_map=cur(2),
        kprev_spec=pl.BlockSpec((BAND, d), prev(1)), vprev_spec=pl.BlockSpec((BAND, d), prev(2)))
    rows_c = cache_k.shape[1]
    bias_s = _rel_bias(rel_table, dseq, rows_c + dseq, rows_c)
    samp = lambda c: (lambda b, i: (b, c))
    cache_spec = pl.BlockSpec((None, rows_c, d), lambda b, i: (b, 0, 0))
    o_attn_s = _attention(
        proj_s, cache_k.reshape(dbsz, rows_c, d), cache_v.reshape(dbsz, rows_c, d), bias_s,
        out_rows=n_s, n_chunks=1, cq=dseq, mask_first=False, grid=(dbsz, 1),
        q_map=samp(0), k_map=samp(1), v_map=samp(2), kprev_spec=cache_spec, vprev_spec=cache_spec)

    lru_w = (conv_w, row(conv_b), _block_diag_groups(lru_wr).astype(_bf16), row(lru_br),
             _block_diag_groups(lru_wi).astype(_bf16), row(lru_bi), row(lru_lambda))
    pad_state = lambda st: jnp.pad(st, ((0, 0), (SUBLANES - (CONV_W - 1), 0), (0, 0)))
    o_lru_p, h_p = _lru(proj, jnp.zeros((bsz, SUBLANES, d), _f32), jnp.zeros((bsz, 1, d), _f32), lru_w,
                        n_batch=bsz, seq=seq, tt=256, row0=0)
    o_lru_s, h_s = _lru(proj_s, pad_state(state_conv), state_lru.reshape(dbsz, 1, d), lru_w,
                        n_batch=dbsz, seq=dseq, tt=dseq, row0=0)

    mix_w = (w_ba.astype(_bf16), w_bl.astype(_bf16), w_o.astype(_bf16), row(norm_ffn), peer_wq.astype(_bf16))
    k1, k2 = keys1.astype(_bf16), keys2.astype(_bf16)

    def ffn(x2, o_attn, o_lru, pr):
        x1, xt, qp = _merge(x2, o_attn, o_lru, pr, *mix_w)
        idx, g_t = _topk(qp, k1, k2)
        w_t = _peer_u(idx, xt, g_t, tab_u)
        peer = _peer_v(idx, w_t, tab_v, d).reshape(x2.shape)
        return _final(x1, peer, row(norm_final))

    y_p = ffn(xp2, o_attn_p, o_lru_p, proj).reshape(bsz, seq, d)
    y_s = ffn(xs2, o_attn_s, o_lru_s, proj_s).reshape(dbsz, dseq, d)

    def window(pr, b, t, rows, c):
        return pr.reshape(b, t, -1)[:, t - rows:, c * d:(c + 1) * d]

    heads = lambda a: a.reshape(a.shape[0], a.shape[1], N_HEADS, HEAD_DIM)
    rows_p = min(BAND, seq)
    keep = CONV_W - 1
    xr_s = jnp.concatenate([state_conv, window(proj_s, dbsz, dseq, dseq, 3)], axis=1)
    return (y_p, y_s, heads(window(proj, bsz, seq, rows_p, 1)), heads(window(proj, bsz, seq, rows_p, 2)),
            window(proj, bsz, seq, keep, 3), h_p.reshape(bsz, d),
            heads(window(proj_s, dbsz, dseq, dseq, 1)), heads(window(proj_s, dbsz, dseq, dseq, 2)),
            xr_s[:, xr_s.shape[1] - keep:], h_s.reshape(dbsz, d))


def kernel(x_prompt, x_sample, cache_k, cache_v, state_conv, state_lru, norm_mix, w_in, rel_table, conv_w, conv_b, lru_wr, lru_br, lru_wi, lru_bi, lru_lambda, w_branch_attn, w_branch_lru, w_out, norm_ffn, peer_wq, peer_keys1, peer_keys2, peer_u, peer_v, norm_final):
    depth = w_in.shape[0]
    assert depth == 1, "the final norm is fused into the single layer's last kernel"
    l = 0
    (y_prompt, y_sample, k_p, v_p, c_p, h_p, k_s, v_s, c_s, h_s) = _layer(
        x_prompt, x_sample, cache_k[l], cache_v[l], state_conv[l], state_lru[l], norm_mix[l], w_in[l],
        rel_table[l], conv_w[l], conv_b[l], lru_wr[l], lru_br[l], lru_wi[l], lru_bi[l], lru_lambda[l],
        w_branch_attn[l], w_branch_lru[l], w_out[l], norm_ffn[l], peer_wq[l], peer_keys1[l], peer_keys2[l],
        _pack_table(peer_u[l]), _pack_table(peer_v[l]), norm_final)
    stack = lambda v: v[None]
    return (y_prompt, y_sample, stack(k_p), stack(v_p), stack(c_p), stack(h_p),
            stack(k_s), stack(v_s), stack(c_s), stack(h_s))
```

```python
import functools

import jax
import jax.numpy as jnp
from jax import lax
from jax.experimental import pallas as pl
from jax.experimental.pallas import tpu as pltpu

CHUNK = 64
LEFT_CHUNKS = 8
BAND = LEFT_CHUNKS * CHUNK
N_HEADS = 16
HEAD_DIM = 64
ATTN_SCALE = HEAD_DIM ** -0.5
MAX_REL = 128
PAST_LEN = 2048
LRU_BLOCKS = 16
CONV_W = 4
LRU_C = 8.0
PEER_HEADS = 8
N_KEYS = 128
TOPK = 16
EPS = 1e-6
NEG_INF = -1e30

LANES = 128
SUBLANES = 8
MXU_DIM = 256
TOKEN_TILE = 128
SLAB = 4
SLAB_STRIDE = 136
VMEM_LIMIT = 56 * 1024 * 1024

_bf16 = jnp.bfloat16
_f32 = jnp.float32


def _row_tile(n_rows):
    for t in (512, 384, 256, 128):
        if n_rows % t == 0:
            return t
    raise ValueError(f"token count {n_rows} must be a multiple of 128")


def _rms(x, g):
    return x * lax.rsqrt(jnp.mean(x * x, axis=-1, keepdims=True) + EPS) * g


def _inproj_kernel(x_ref, g_ref, w_ref, o_ref):
    d = x_ref.shape[1]
    xn = _rms(x_ref[...], g_ref[...]).astype(_bf16)
    for j in range(w_ref.shape[1] // d):
        cols = slice(j * d, (j + 1) * d)
        o_ref[:, cols] = jnp.dot(xn, w_ref[:, cols], preferred_element_type=_f32)


def _inproj(x, g, w):
    n, d = x.shape
    cols = w.shape[1]
    tm = 256 if n % 256 == 0 else TOKEN_TILE
    return pl.pallas_call(
        _inproj_kernel,
        grid=(n // tm,),
        in_specs=[pl.BlockSpec((tm, d), lambda i: (i, 0)),
                  pl.BlockSpec((1, d), lambda i: (0, 0)),
                  pl.BlockSpec((d, cols), lambda i: (0, 0), pipeline_mode=pl.Buffered(1))],
        out_specs=pl.BlockSpec((tm, cols), lambda i: (i, 0)),
        out_shape=jax.ShapeDtypeStruct((n, cols), _f32),
        compiler_params=pltpu.CompilerParams(
            dimension_semantics=("parallel",), vmem_limit_bytes=VMEM_LIMIT),
        name="inproj",
    )(x, g, w)


def _attn_kernel(q_ref, kp_ref, kc_ref, vp_ref, vc_ref, bias_ref, o_ref, kcat, vcat,
                 *, n_chunks, cq, mask_first):
    nq = n_chunks * cq
    span = BAND + cq
    kcat[0:BAND, :] = kp_ref[...].astype(_bf16)
    kcat[BAND:BAND + nq, :] = kc_ref[...].astype(_bf16)
    vcat[0:BAND, :] = vp_ref[...].astype(_bf16)
    vcat[BAND:BAND + nq, :] = vc_ref[...].astype(_bf16)
    first = pl.program_id(1) == 0
    pair = 2 * HEAD_DIM
    low_head = lax.broadcasted_iota(jnp.int32, (cq, pair), 1) < HEAD_DIM

    def chunk(cl, carry):
        r0 = pl.multiple_of(cl * cq, cq)
        q = q_ref[pl.ds(r0, cq), :] * ATTN_SCALE
        kw = kcat[pl.ds(r0, span), :]
        vw = vcat[pl.ds(r0, span), :]
        if mask_first:
            col = lax.broadcasted_iota(jnp.int32, (cq, span), 1)
            valid = jnp.logical_or(jnp.logical_not(first), col + r0 >= BAND)
        scores = []
        for pr in range(N_HEADS // 2):
            sl = slice(pr * pair, (pr + 1) * pair)
            for keep_low in (True, False):
                qh = (jnp.where(low_head, q[:, sl], 0.0) if keep_low
                      else jnp.where(low_head, 0.0, q[:, sl])).astype(_bf16)
                scores.append(lax.dot_general(qh, kw[:, sl], (((1,), (1,)), ((), ())),
                                              preferred_element_type=_f32))
        probs, inv = [], []
        for h in range(N_HEADS):
            s = scores[h] + bias_ref[h]
            if mask_first:
                s = jnp.where(valid, s, NEG_INF)
            p = jnp.exp(s - jnp.max(s, axis=-1, keepdims=True))
            inv.append(1.0 / jnp.sum(p, axis=-1, keepdims=True))
            probs.append(p.astype(_bf16))
        outs = []
        for pr in range(N_HEADS // 2):
            vp = vw[:, pr * pair:(pr + 1) * pair]
            o_lo = jnp.dot(probs[2 * pr], vp, preferred_element_type=_f32) * inv[2 * pr]
            o_hi = jnp.dot(probs[2 * pr + 1], vp, preferred_element_type=_f32) * inv[2 * pr + 1]
            outs.append(jnp.where(low_head, o_lo, o_hi))
        o_ref[pl.ds(r0, cq), :] = jnp.concatenate(outs, axis=-1).astype(o_ref.dtype)
        return carry

    lax.fori_loop(0, n_chunks, chunk, 0)


def _attention(q_src, kprev_src, vprev_src, bias, *, out_rows, n_chunks, cq, mask_first, grid,
               q_map, k_map, v_map, kprev_spec, vprev_spec):
    nq = n_chunks * cq
    d = N_HEADS * HEAD_DIM
    span = BAND + cq
    kern = functools.partial(_attn_kernel, n_chunks=n_chunks, cq=cq, mask_first=mask_first)
    return pl.pallas_call(
        kern,
        grid=grid,
        in_specs=[pl.BlockSpec((nq, d), q_map),
                  kprev_spec,
                  pl.BlockSpec((nq, d), k_map),
                  vprev_spec,
                  pl.BlockSpec((nq, d), v_map),
                  pl.BlockSpec((N_HEADS, cq, span), lambda b, i: (0, 0, 0))],
        out_specs=pl.BlockSpec((nq, d), lambda b, i, g=grid: (b * g[1] + i, 0)),
        out_shape=jax.ShapeDtypeStruct((out_rows, d), _bf16),
        scratch_shapes=[pltpu.VMEM((BAND + nq, d), _bf16), pltpu.VMEM((BAND + nq, d), _bf16)],
        compiler_params=pltpu.CompilerParams(
            dimension_semantics=("parallel", "arbitrary"), vmem_limit_bytes=VMEM_LIMIT),
        name="band_attention",
    )(q_src, kprev_src, q_src, vprev_src, q_src, bias)


def _shift_rows(x, s, fill, row):
    return jnp.where(row >= s, pltpu.roll(x, s, axis=0), fill)


def _lru_kernel(x_ref, gb_ref, cinit_ref, h0_ref, cw_ref, cb_ref, wr_ref, br_ref, wi_ref, bi_ref,
                lam_ref, o_ref, hl_ref, xbuf, hcar, *, tt):
    d = x_ref.shape[-1]

    @pl.when(pl.program_id(1) == 0)
    def _():
        xbuf[0:SUBLANES, :] = cinit_ref[...]
        hcar[...] = h0_ref[...]

    xbuf[SUBLANES:SUBLANES + tt, :] = x_ref[...]
    base = SUBLANES - (CONV_W - 1)
    xc = cb_ref[...]
    for k in range(CONV_W):
        xc = xc + xbuf[pl.ds(base + k, tt), :] * cw_ref[k:k + 1, :]
    xbuf[0:SUBLANES, :] = xbuf[tt:tt + SUBLANES, :]

    xcb = xc.astype(_bf16)
    n_groups = d // MXU_DIM
    r_parts, i_parts = [], []
    for g in range(n_groups):
        xs = xcb[:, g * MXU_DIM:(g + 1) * MXU_DIM]
        r_parts.append(jnp.dot(xs, wr_ref[g], preferred_element_type=_f32))
        i_parts.append(jnp.dot(xs, wi_ref[g], preferred_element_type=_f32))
    r = jax.nn.sigmoid(jnp.concatenate(r_parts, axis=-1) + br_ref[...])
    gi = jax.nn.sigmoid(jnp.concatenate(i_parts, axis=-1) + bi_ref[...])
    z = -lam_ref[...]
    softplus = jnp.maximum(z, 0.0) + jnp.log(1.0 + jnp.exp(-jnp.abs(z)))
    log_a = (-LRU_C) * r * softplus
    a = jnp.exp(log_a)
    b = jnp.sqrt(1.0 - a * a) * (gi * xc)

    row = lax.broadcasted_iota(jnp.int32, (tt, d), 0)
    s = 1
    while s < tt:
        b = a * _shift_rows(b, s, 0.0, row) + b
        a = a * _shift_rows(a, s, 1.0, row)
        s *= 2
    h = a * hcar[...] + b
    h_last = h[tt - 1:tt, :]
    hcar[...] = h_last
    hl_ref[...] = h_last
    o_ref[...] = (h * jax.nn.gelu(gb_ref[...])).astype(o_ref.dtype)


def _lru(proj, conv_init, h0, weights, *, n_batch, seq, tt, row0):
    d = conv_init.shape[-1]
    steps = seq // tt
    blk0 = row0 // tt
    (cw, cb, wr, br, wi, bi, lam) = weights
    const2 = lambda b, i: (0, 0)
    const3 = lambda b, i: (0, 0, 0)
    kern = functools.partial(_lru_kernel, tt=tt)
    return pl.pallas_call(
        kern,
        grid=(n_batch, steps),
        in_specs=[pl.BlockSpec((tt, d), lambda b, i: (blk0 + b * steps + i, 3)),
                  pl.BlockSpec((tt, d), lambda b, i: (blk0 + b * steps + i, 4)),
                  pl.BlockSpec((None, SUBLANES, d), lambda b, i: (b, 0, 0)),
                  pl.BlockSpec((None, 1, d), lambda b, i: (b, 0, 0)),
                  pl.BlockSpec((CONV_W, d), const2),
                  pl.BlockSpec((1, d), const2),
                  pl.BlockSpec(wr.shape, const3),
                  pl.BlockSpec((1, d), const2),
                  pl.BlockSpec(wi.shape, const3),
                  pl.BlockSpec((1, d), const2),
                  pl.BlockSpec((1, d), const2)],
        out_specs=[pl.BlockSpec((tt, d), lambda b, i: (b * steps + i, 0)),
                   pl.BlockSpec((None, 1, d), lambda b, i: (b, 0, 0))],
        out_shape=[jax.ShapeDtypeStruct((n_batch * seq, d), _bf16),
                   jax.ShapeDtypeStruct((n_batch, 1, d), _f32)],
        scratch_shapes=[pltpu.VMEM((tt + 2 * SUBLANES, d), _f32), pltpu.VMEM((1, d), _f32)],
        compiler_params=pltpu.CompilerParams(
            dimension_semantics=("parallel", "arbitrary"), vmem_limit_bytes=VMEM_LIMIT),
        name="conv_rglru",
    )(proj, proj, conv_init, h0, cw, cb, wr, br, wi, bi, lam)


def _merge_kernel(x_ref, oa_ref, ol_ref, ga_ref, gr_ref, wba_ref, wbl_ref, wo_ref, gn_ref, wq_ref,
                  x1_ref, xt_ref, qp_ref):
    ya = jnp.dot(oa_ref[...], wba_ref[...], preferred_element_type=_f32)
    yl = jnp.dot(ol_ref[...], wbl_ref[...], preferred_element_type=_f32)
    merged = jax.nn.sigmoid(ga_ref[...]) * ya + jax.nn.sigmoid(gr_ref[...]) * yl
    x1 = x_ref[...] + jnp.dot(merged.astype(_bf16), wo_ref[...], preferred_element_type=_f32)
    x1_ref[...] = x1
    xt = _rms(x1, gn_ref[...])
    xt_ref[...] = xt
    qp_ref[...] = jnp.dot(xt.astype(_bf16), wq_ref[...], preferred_element_type=_f32).astype(_bf16)


def _merge(x, o_attn, o_lru, proj, w_ba, w_bl, w_o, g_ffn, wq):
    n, d = x.shape
    tm = _row_tile(n)
    nq = wq.shape[1]
    row = lambda i: (i, 0)
    const = lambda i: (0, 0)
    return pl.pallas_call(
        _merge_kernel,
        grid=(n // tm,),
        in_specs=[pl.BlockSpec((tm, d), row),
                  pl.BlockSpec((tm, d), row),
                  pl.BlockSpec((tm, d), row),
                  pl.BlockSpec((tm, d), lambda i: (i, 5)),
                  pl.BlockSpec((tm, d), lambda i: (i, 6)),
                  pl.BlockSpec((d, d), const, pipeline_mode=pl.Buffered(1)),
                  pl.BlockSpec((d, d), const, pipeline_mode=pl.Buffered(1)),
                  pl.BlockSpec((d, d), const, pipeline_mode=pl.Buffered(1)),
                  pl.BlockSpec((1, d), const),
                  pl.BlockSpec((d, nq), const, pipeline_mode=pl.Buffered(1))],
        out_specs=[pl.BlockSpec((tm, d), row), pl.BlockSpec((tm, d), row), pl.BlockSpec((tm, nq), row)],
        out_shape=[jax.ShapeDtypeStruct((n, d), _f32), jax.ShapeDtypeStruct((n, d), _f32),
                   jax.ShapeDtypeStruct((n, nq), _bf16)],
        compiler_params=pltpu.CompilerParams(
            dimension_semantics=("parallel",), vmem_limit_bytes=VMEM_LIMIT),
        name="merge_outproj_peerq",
    )(x, o_attn, o_lru, proj, proj, w_ba, w_bl, w_o, g_ffn, wq)


def _top16(s, payload=None):
    n_rows = s.shape[0]
    row = lax.broadcasted_iota(jnp.int32, s.shape, 0).astype(_f32)
    vals, picks = [], []
    for _ in range(TOPK):
        m = jnp.max(s, axis=0, keepdims=True)
        am = jnp.min(jnp.where(s == m, row, float(n_rows)), axis=0, keepdims=True)
        hit = row == am
        vals.append(m)
        if payload is None:
            picks.append(am)
        else:
            picks.append(jnp.max(jnp.where(hit, payload, -1.0), axis=0, keepdims=True))
        s = jnp.where(hit, -jnp.inf, s)
    return jnp.concatenate(vals, axis=0), jnp.concatenate(picks, axis=0)


def _candidates(a, b, mask_out):
    j8 = lax.broadcasted_iota(jnp.int32, (SUBLANES, a.shape[1]), 0)
    pieces = [a[0:1, :] + b]
    for i in range(1, SUBLANES):
        piece = a[i:i + 1, :] + b[0:SUBLANES, :]
        if mask_out:
            piece = jnp.where(j8 < TOPK // (i + 1), piece, -jnp.inf)
        pieces.append(piece)
    pieces.append(a[SUBLANES:TOPK, :] + b[0:1, :])
    return jnp.concatenate(pieces, axis=0)


def _topk_kernel(qp_ref, k1_ref, k2_ref, idx_ref, g_ref):
    nt = (((1,), (1,)), ((), ()))
    picks = []
    for h in range(PEER_HEADS):
        c0 = h * 2 * N_KEYS
        q1 = qp_ref[:, c0:c0 + N_KEYS]
        q2 = qp_ref[:, c0 + N_KEYS:c0 + 2 * N_KEYS]
        s1 = lax.dot_general(k1_ref[h], q1, nt, preferred_element_type=_f32)
        s2 = lax.dot_general(k2_ref[h], q2, nt, preferred_element_type=_f32)
        v1, i1 = _top16(s1)
        v2, i2 = _top16(s2)
        cand_s = _candidates(v1, v2, True)
        cand_i = _candidates(i1 * float(N_KEYS), i2, False)
        top_s, e_idx = _top16(cand_s, cand_i)
        p = jnp.exp(top_s - jnp.max(top_s, axis=0, keepdims=True))
        g = p / jnp.sum(p, axis=0, keepdims=True)
        picks.append(e_idx)
        g_ref[h * TOPK:(h + 1) * TOPK, :] = g
    idx_ref[...] = jnp.concatenate(picks, axis=0).T.astype(jnp.int32) * SLAB


def _topk(qp, keys1, keys2):
    n = qp.shape[0]
    nsel = PEER_HEADS * TOPK
    const3 = lambda i: (0, 0, 0)
    return pl.pallas_call(
        _topk_kernel,
        grid=(n // TOKEN_TILE,),
        in_specs=[pl.BlockSpec((TOKEN_TILE, qp.shape[1]), lambda i: (i, 0)),
                  pl.BlockSpec(keys1.shape, const3),
                  pl.BlockSpec(keys2.shape, const3)],
        out_specs=[pl.BlockSpec((TOKEN_TILE, nsel), lambda i: (i, 0)),
                   pl.BlockSpec((nsel, TOKEN_TILE), lambda i: (0, i))],
        out_shape=[jax.ShapeDtypeStruct((n, nsel), jnp.int32), jax.ShapeDtypeStruct((nsel, n), _f32)],
        compiler_params=pltpu.CompilerParams(
            dimension_semantics=("parallel",), vmem_limit_bytes=VMEM_LIMIT),
        name="peer_topk",
    )(qp, keys1, keys2)


def _pack_table(t):
    e, d = t.shape
    bits = lax.bitcast_convert_type(t.astype(_bf16), jnp.uint16).astype(jnp.uint32)
    words = bits[:, :d // 2] | (bits[:, d // 2:] << 16)
    return words.reshape(e * SLAB, LANES)


def _unpack(words):
    lo = pltpu.bitcast(words << 16, _f32)
    hi = pltpu.bitcast(words & jnp.uint32(0xFFFF0000), _f32)
    return lo, hi


def _gather_transposed(idx_ref, t, tab_ref, tile, slice_row):
    row = idx_ref.at[t] if slice_row else None
    for j in range(idx_ref.shape[1]):
        r = pl.multiple_of(row[j] if slice_row else idx_ref[t, j], SLAB)
        tile[pl.ds(j, SLAB, stride=SLAB_STRIDE), :] = tab_ref[pl.ds(r, SLAB), :]


def _tile_chunk(tile, s, nsel):
    return _unpack(tile[s * SLAB_STRIDE:s * SLAB_STRIDE + nsel, :])


def _split_bf16(x):
    hi = x.astype(_bf16)
    return hi, (x - hi.astype(_f32)).astype(_bf16)


def _mxu_right(x, rhs):
    hi, lo = _split_bf16(x)
    return (jnp.dot(hi, rhs, preferred_element_type=_f32) + jnp.dot(lo, rhs, preferred_element_type=_f32))


REDUCE_GROUP = 32


def _peer_u_kernel(idx_ref, xt_ref, g_ref, tab_ref, w_ref, tile_a, tile_b, parts):
    nsel = idx_ref.shape[1]

    def partial_dots(t, tile):
        x = xt_ref[t]
        acc = jnp.zeros((nsel, LANES), _f32)
        for s in range(SLAB):
            lo, hi = _tile_chunk(tile, s, nsel)
            acc = acc + lo * x[s:s + 1, :] + hi * x[SLAB + s:SLAB + s + 1, :]
        return acc

    tile_b[...] = jnp.zeros_like(tile_b)

    def two_tokens(i, carry):
        t = 2 * i
        t_prev = jnp.maximum(t - 1, 0)
        parts[t_prev] = partial_dots(t_prev, tile_b)
        _gather_transposed(idx_ref, t, tab_ref, tile_a, slice_row=False)
        _gather_transposed(idx_ref, t + 1, tab_ref, tile_b, slice_row=False)
        parts[t] = partial_dots(t, tile_a)
        return carry

    lax.fori_loop(0, TOKEN_TILE // 2, two_tokens, 0)
    parts[TOKEN_TILE - 1] = partial_dots(TOKEN_TILE - 1, tile_b)

    lane = lax.broadcasted_iota(jnp.int32, (nsel, TOKEN_TILE), 1)
    ones = jnp.ones((LANES, LANES), _bf16)

    def reduce_group(gi, act):
        for k in range(REDUCE_GROUP):
            t = gi * REDUCE_GROUP + k
            act = jnp.where(lane == t, _mxu_right(parts[t], ones), act)
        return act

    act = lax.fori_loop(0, TOKEN_TILE // REDUCE_GROUP, reduce_group, jnp.zeros((nsel, TOKEN_TILE), _f32))
    w_ref[...] = g_ref[...] * jax.nn.gelu(act)


def _peer_u(idx, xt, g_t, table):
    n, d = xt.shape
    nsel = idx.shape[1]
    xt = xt.reshape(n, d // LANES, LANES)
    return pl.pallas_call(
        _peer_u_kernel,
        grid=(n // TOKEN_TILE,),
        in_specs=[pl.BlockSpec((TOKEN_TILE, nsel), lambda i: (i, 0), memory_space=pltpu.SMEM),
                  pl.BlockSpec((TOKEN_TILE, d // LANES, LANES), lambda i: (i, 0, 0)),
                  pl.BlockSpec((nsel, TOKEN_TILE), lambda i: (0, i)),
                  _resident_table_spec(table)],
        out_specs=pl.BlockSpec((nsel, TOKEN_TILE), lambda i: (0, i)),
        out_shape=jax.ShapeDtypeStruct((nsel, n), _f32),
        scratch_shapes=[pltpu.VMEM((SLAB * SLAB_STRIDE, LANES), jnp.uint32),
                        pltpu.VMEM((SLAB * SLAB_STRIDE, LANES), jnp.uint32),
                        pltpu.VMEM((TOKEN_TILE, nsel, LANES), _f32)],
        compiler_params=pltpu.CompilerParams(
            dimension_semantics=("arbitrary",), vmem_limit_bytes=VMEM_LIMIT),
        name="peer_u_pass",
    )(idx, xt, g_t, table)


def _resident_table_spec(table):
    return pl.BlockSpec(table.shape, lambda i: (0, 0), pipeline_mode=pl.Buffered(1))


def _peer_v_kernel(idx_ref, w_ref, tab_ref, o_ref, tile_a, tile_b, wcol_b):
    nsel = idx_ref.shape[1]
    lane = lax.broadcasted_iota(jnp.int32, (nsel, TOKEN_TILE), 1)

    def weight_column(t):
        w_col = jnp.sum(jnp.where(lane == t, w_ref[...], 0.0), axis=1, keepdims=True)
        return jnp.broadcast_to(w_col, (nsel, LANES))

    def weighted_sum(t, tile, w_col):
        rows_lo, rows_hi = [], []
        for s in range(SLAB):
            lo, hi = _tile_chunk(tile, s, nsel)
            rows_lo.append(jnp.sum(lo * w_col, axis=0, keepdims=True))
            rows_hi.append(jnp.sum(hi * w_col, axis=0, keepdims=True))
        o_ref[t] = jnp.concatenate(rows_lo + rows_hi, axis=0)

    tile_b[...] = jnp.zeros_like(tile_b)
    wcol_b[...] = jnp.zeros_like(wcol_b)

    def two_tokens(i, carry):
        t = 2 * i
        w_a = weight_column(t)
        w_b = weight_column(t + 1)
        weighted_sum(jnp.maximum(t - 1, 0), tile_b, wcol_b[...])
        _gather_transposed(idx_ref, t, tab_ref, tile_a, slice_row=True)
        _gather_transposed(idx_ref, t + 1, tab_ref, tile_b, slice_row=True)
        weighted_sum(t, tile_a, w_a)
        wcol_b[...] = w_b
        return carry

    lax.fori_loop(0, TOKEN_TILE // 2, two_tokens, 0)
    weighted_sum(TOKEN_TILE - 1, tile_b, wcol_b[...])


def _peer_v(idx, w_t, table, d):
    n, nsel = idx.shape
    return pl.pallas_call(
        _peer_v_kernel,
        grid=(n // TOKEN_TILE,),
        in_specs=[pl.BlockSpec((TOKEN_TILE, nsel), lambda i: (i, 0), memory_space=pltpu.SMEM),
                  pl.BlockSpec((nsel, TOKEN_TILE), lambda i: (0, i)),
                  _resident_table_spec(table)],
        out_specs=pl.BlockSpec((TOKEN_TILE, d // LANES, LANES), lambda i: (i, 0, 0)),
        out_shape=jax.ShapeDtypeStruct((n, d // LANES, LANES), _f32),
        scratch_shapes=[pltpu.VMEM((SLAB * SLAB_STRIDE, LANES), jnp.uint32),
                        pltpu.VMEM((SLAB * SLAB_STRIDE, LANES), jnp.uint32),
                        pltpu.VMEM((nsel, LANES), _f32)],
        compiler_params=pltpu.CompilerParams(
            dimension_semantics=("arbitrary",), vmem_limit_bytes=VMEM_LIMIT),
        name="peer_v_pass",
    )(idx, w_t, table)


def _final_kernel(x1_ref, p_ref, g_ref, y_ref):
    y_ref[...] = _rms(x1_ref[...] + p_ref[...], g_ref[...])


def _final(x1, peer, g):
    n, d = x1.shape
    tm = _row_tile(n)
    row = lambda i: (i, 0)
    return pl.pallas_call(
        _final_kernel,
        grid=(n // tm,),
        in_specs=[pl.BlockSpec((tm, d), row), pl.BlockSpec((tm, d), row), pl.BlockSpec((1, d), lambda i: (0, 0))],
        out_specs=pl.BlockSpec((tm, d), row),
        out_shape=jax.ShapeDtypeStruct((n, d), _f32),
        compiler_params=pltpu.CompilerParams(
            dimension_semantics=("parallel",), vmem_limit_bytes=VMEM_LIMIT),
        name="residual_final_norm",
    )(x1, peer, g)


def _rel_bias(rel_table, n_q, n_k, offset):
    m = jnp.arange(n_k + n_q - 1)
    diag = rel_table[:, jnp.clip(offset + n_q - 1 - m, -MAX_REL, MAX_REL) + MAX_REL].astype(_f32)
    return jnp.stack([diag[:, n_q - 1 - q:n_q - 1 - q + n_k] for q in range(n_q)], axis=1)


def _block_diag_groups(w):
    nb, bi, bo = w.shape
    per = MXU_DIM // bi
    w = w.reshape(nb // per, per, bi, bo)
    eye = jnp.eye(per, dtype=w.dtype)
    return jnp.einsum("gpio,pq->gpiqo", w, eye).reshape(nb // per, per * bi, per * bo)


def _layer(xp, xs, cache_k, cache_v, state_conv, state_lru, norm_mix, w_in, rel_table, conv_w, conv_b,
           lru_wr, lru_br, lru_wi, lru_bi, lru_lambda, w_ba, w_bl, w_o, norm_ffn, peer_wq, keys1, keys2,
           tab_u, tab_v, norm_final):
    bsz, seq, d = xp.shape
    dbsz, dseq, _ = xs.shape
    n_p, n_s = bsz * seq, dbsz * dseq
    xp2, xs2 = xp.reshape(n_p, d), xs.reshape(n_s, d)
    row = lambda v: v.reshape(1, -1)

    w_in_b = w_in.astype(_bf16)
    proj = _inproj(xp2, row(norm_mix), w_in_b)
    proj_s = _inproj(xs2, row(norm_mix), w_in_b)

    tiles = seq // BAND
    bias_p = _rel_bias(rel_table, CHUNK, BAND + CHUNK, BAND)
    prev = lambda c: (lambda b, i: (b * tiles + jnp.maximum(i - 1, 0), c))
    cur = lambda c: (lambda b, i: (b * tiles + i, c))
    o_attn_p = _attention(
        proj, proj, proj, bias_p, out_rows=n_p, n_chunks=BAND // CHUNK, cq=CHUNK, mask_first=True,
        grid=(bsz, tiles), q_map=cur(0), k_map=cur(1), v_map=cur(2),
        kprev_spec=pl.BlockSpec((BAND, d), prev(1)), vprev_spec=pl.BlockSpec((BAND, d), prev(2)))
    rows_c = cache_k.shape[1]
    bias_s = _rel_bias(rel_table, dseq, rows_c + dseq, rows_c)
    samp = lambda c: (lambda b, i: (b, c))
    cache_spec = pl.BlockSpec((None, rows_c, d), lambda b, i: (b, 0, 0))
    o_attn_s = _attention(
        proj_s, cache_k.reshape(dbsz, rows_c, d), cache_v.reshape(dbsz, rows_c, d), bias_s,
        out_rows=n_s, n_chunks=1, cq=dseq, mask_first=False, grid=(dbsz, 1),
        q_map=samp(0), k_map=samp(1), v_map=samp(2), kprev_spec=cache_spec, vprev_spec=cache_spec)

    lru_w = (conv_w, row(conv_b), _block_diag_groups(lru_wr).astype(_bf16), row(lru_br),
             _block_diag_groups(lru_wi).astype(_bf16), row(lru_bi), row(lru_lambda))
    pad_state = lambda st: jnp.pad(st, ((0, 0), (SUBLANES - (CONV_W - 1), 0), (0, 0)))
    o_lru_p, h_p = _lru(proj, jnp.zeros((bsz, SUBLANES, d), _f32), jnp.zeros((bsz, 1, d), _f32), lru_w,
                        n_batch=bsz, seq=seq, tt=256, row0=0)
    o_lru_s, h_s = _lru(proj_s, pad_state(state_conv), state_lru.reshape(dbsz, 1, d), lru_w,
                        n_batch=dbsz, seq=dseq, tt=dseq, row0=0)

    mix_w = (w_ba.astype(_bf16), w_bl.astype(_bf16), w_o.astype(_bf16), row(norm_ffn), peer_wq.astype(_bf16))
    k1, k2 = keys1.astype(_bf16), keys2.astype(_bf16)

    def ffn(x2, o_attn, o_lru, pr):
        x1, xt, qp = _merge(x2, o_attn, o_lru, pr, *mix_w)
        idx, g_t = _topk(qp, k1, k2)
        w_t = _peer_u(idx, xt, g_t, tab_u)
        peer = _peer_v(idx, w_t, tab_v, d).reshape(x2.shape)
        return _final(x1, peer, row(norm_final))

    y_p = ffn(xp2, o_attn_p, o_lru_p, proj).reshape(bsz, seq, d)
    y_s = ffn(xs2, o_attn_s, o_lru_s, proj_s).reshape(dbsz, dseq, d)

    def window(pr, b, t, rows, c):
        return pr.reshape(b, t, -1)[:, t - rows:, c * d:(c + 1) * d]

    heads = lambda a: a.reshape(a.shape[0], a.shape[1], N_HEADS, HEAD_DIM)
    rows_p = min(BAND, seq)
    keep = CONV_W - 1
    xr_s = jnp.concatenate([state_conv, window(proj_s, dbsz, dseq, dseq, 3)], axis=1)
    return (y_p, y_s, heads(window(proj, bsz, seq, rows_p, 1)), heads(window(proj, bsz, seq, rows_p, 2)),
            window(proj, bsz, seq, keep, 3), h_p.reshape(bsz, d),
            heads(window(proj_s, dbsz, dseq, dseq, 1)), heads(window(proj_s, dbsz, dseq, dseq, 2)),
            xr_s[:, xr_s.shape[1] - keep:], h_s.reshape(dbsz, d))


def kernel(x_prompt, x_sample, cache_k, cache_v, state_conv, state_lru, norm_mix, w_in, rel_table, conv_w, conv_b, lru_wr, lru_br, lru_wi, lru_bi, lru_lambda, w_branch_attn, w_branch_lru, w_out, norm_ffn, peer_wq, peer_keys1, peer_keys2, peer_u, peer_v, norm_final):
    depth = w_in.shape[0]
    assert depth == 1, "the final norm is fused into the single layer's last kernel"
    l = 0
    (y_prompt, y_sample, k_p, v_p, c_p, h_p, k_s, v_s, c_s, h_s) = _layer(
        x_prompt, x_sample, cache_k[l], cache_v[l], state_conv[l], state_lru[l], norm_mix[l], w_in[l],
        rel_table[l], conv_w[l], conv_b[l], lru_wr[l], lru_br[l], lru_wi[l], lru_bi[l], lru_lambda[l],
        w_branch_attn[l], w_branch_lru[l], w_out[l], norm_ffn[l], peer_wq[l], peer_keys1[l], peer_keys2[l],
        _pack_table(peer_u[l]), _pack_table(peer_v[l]), norm_final)
    stack = lambda v: v[None]
    return (y_prompt, y_sample, stack(k_p), stack(v_p), stack(c_p), stack(h_p),
            stack(k_s), stack(v_s), stack(c_s), stack(h_s))
```

```python
import functools

import jax
import jax.numpy as jnp
from jax import lax
from jax.experimental import pallas as pl
from jax.experimental.pallas import tpu as pltpu

CHUNK = 64
LEFT_CHUNKS = 8
BAND = LEFT_CHUNKS * CHUNK
N_HEADS = 16
HEAD_DIM = 64
ATTN_SCALE = HEAD_DIM ** -0.5
MAX_REL = 128
PAST_LEN = 2048
LRU_BLOCKS = 16
CONV_W = 4
LRU_C = 8.0
PEER_HEADS = 8
N_KEYS = 128
TOPK = 16
EPS = 1e-6
NEG_INF = -1e30

LANES = 128
SUBLANES = 8
MXU_DIM = 256
TOKEN_TILE = 128
SLAB = 4
SLAB_STRIDE = 136
VMEM_LIMIT = 56 * 1024 * 1024

_bf16 = jnp.bfloat16
_f32 = jnp.float32


def _row_tile(n_rows):
    for t in (512, 384, 256, 128):
        if n_rows % t == 0:
            return t
    raise ValueError(f"token count {n_rows} must be a multiple of 128")


def _rms(x, g):
    return x * lax.rsqrt(jnp.mean(x * x, axis=-1, keepdims=True) + EPS) * g


F32_BLOCKS = (1, 2, 3)
BF16_BLOCKS = (0, 4, 5, 6)
K_COL, V_COL, XR_COL = 0, 1, 2
Q_COL, GB_COL, GA_COL, GR_COL = 0, 1, 2, 3


def _inproj_kernel(x_ref, g_ref, w_ref, o32_ref, o16_ref):
    d = x_ref.shape[1]
    xn = _rms(x_ref[...], g_ref[...]).astype(_bf16)
    for j in range(w_ref.shape[1] // d):
        y = jnp.dot(xn, w_ref[:, j * d:(j + 1) * d], preferred_element_type=_f32)
        if j in F32_BLOCKS:
            c = F32_BLOCKS.index(j)
            o32_ref[:, c * d:(c + 1) * d] = y
        else:
            c = BF16_BLOCKS.index(j)
            o16_ref[:, c * d:(c + 1) * d] = y.astype(_bf16)


def _inproj(x, g, w):
    n, d = x.shape
    cols = w.shape[1]
    assert cols == (len(F32_BLOCKS) + len(BF16_BLOCKS)) * d
    tm = 256 if n % 256 == 0 else TOKEN_TILE
    n32, n16 = len(F32_BLOCKS) * d, len(BF16_BLOCKS) * d
    return pl.pallas_call(
        _inproj_kernel,
        grid=(n // tm,),
        in_specs=[pl.BlockSpec((tm, d), lambda i: (i, 0)),
                  pl.BlockSpec((1, d), lambda i: (0, 0)),
                  pl.BlockSpec((d, cols), lambda i: (0, 0), pipeline_mode=pl.Buffered(1))],
        out_specs=[pl.BlockSpec((tm, n32), lambda i: (i, 0)), pl.BlockSpec((tm, n16), lambda i: (i, 0))],
        out_shape=[jax.ShapeDtypeStruct((n, n32), _f32), jax.ShapeDtypeStruct((n, n16), _bf16)],
        compiler_params=pltpu.CompilerParams(
            dimension_semantics=("parallel",), vmem_limit_bytes=VMEM_LIMIT),
        name="inproj",
    )(x, g, w)


def _attn_kernel(q_ref, kp_ref, kc_ref, vp_ref, vc_ref, bias_ref, o_ref, kcat, vcat,
                 *, n_chunks, cq, mask_first):
    nq = n_chunks * cq
    span = BAND + cq
    kcat[0:BAND, :] = kp_ref[...].astype(_bf16)
    kcat[BAND:BAND + nq, :] = kc_ref[...].astype(_bf16)
    vcat[0:BAND, :] = vp_ref[...].astype(_bf16)
    vcat[BAND:BAND + nq, :] = vc_ref[...].astype(_bf16)
    first = pl.program_id(1) == 0
    pair = 2 * HEAD_DIM
    low_head = lax.broadcasted_iota(jnp.int32, (cq, pair), 1) < HEAD_DIM

    def chunk(cl, carry):
        r0 = pl.multiple_of(cl * cq, cq)
        q = q_ref[pl.ds(r0, cq), :] * ATTN_SCALE
        kw = kcat[pl.ds(r0, span), :]
        vw = vcat[pl.ds(r0, span), :]
        if mask_first:
            col = lax.broadcasted_iota(jnp.int32, (cq, span), 1)
            valid = jnp.logical_or(jnp.logical_not(first), col + r0 >= BAND)
        scores = []
        for pr in range(N_HEADS // 2):
            sl = slice(pr * pair, (pr + 1) * pair)
            for keep_low in (True, False):
                qh = (jnp.where(low_head, q[:, sl], 0.0) if keep_low
                      else jnp.where(low_head, 0.0, q[:, sl])).astype(_bf16)
                scores.append(lax.dot_general(qh, kw[:, sl], (((1,), (1,)), ((), ())),
                                              preferred_element_type=_f32))
        probs, inv = [], []
        for h in range(N_HEADS):
            s = scores[h] + bias_ref[h]
            if mask_first:
                s = jnp.where(valid, s, NEG_INF)
            p = jnp.exp(s - jnp.max(s, axis=-1, keepdims=True))
            inv.append(1.0 / jnp.sum(p, axis=-1, keepdims=True))
            probs.append(p.astype(_bf16))
        outs = []
        for pr in range(N_HEADS // 2):
            vp = vw[:, pr * pair:(pr + 1) * pair]
            o_lo = jnp.dot(probs[2 * pr], vp, preferred_element_type=_f32) * inv[2 * pr]
            o_hi = jnp.dot(probs[2 * pr + 1], vp, preferred_element_type=_f32) * inv[2 * pr + 1]
            outs.append(jnp.where(low_head, o_lo, o_hi))
        o_ref[pl.ds(r0, cq), :] = jnp.concatenate(outs, axis=-1).astype(o_ref.dtype)
        return carry

    lax.fori_loop(0, n_chunks, chunk, 0)


def _attention(q_src, kv_src, kprev_src, vprev_src, bias, *, out_rows, n_chunks, cq, mask_first, grid,
               q_map, k_map, v_map, kprev_spec, vprev_spec):
    nq = n_chunks * cq
    d = N_HEADS * HEAD_DIM
    span = BAND + cq
    kern = functools.partial(_attn_kernel, n_chunks=n_chunks, cq=cq, mask_first=mask_first)
    return pl.pallas_call(
        kern,
        grid=grid,
        in_specs=[pl.BlockSpec((nq, d), q_map),
                  kprev_spec,
                  pl.BlockSpec((nq, d), k_map),
                  vprev_spec,
                  pl.BlockSpec((nq, d), v_map),
                  pl.BlockSpec((N_HEADS, cq, span), lambda b, i: (0, 0, 0))],
        out_specs=pl.BlockSpec((nq, d), lambda b, i, g=grid: (b * g[1] + i, 0)),
        out_shape=jax.ShapeDtypeStruct((out_rows, d), _bf16),
        scratch_shapes=[pltpu.VMEM((BAND + nq, d), _bf16), pltpu.VMEM((BAND + nq, d), _bf16)],
        compiler_params=pltpu.CompilerParams(
            dimension_semantics=("parallel", "arbitrary"), vmem_limit_bytes=VMEM_LIMIT),
        name="band_attention",
    )(q_src, kprev_src, kv_src, vprev_src, kv_src, bias)


def _shift_rows(x, s, fill, row):
    return jnp.where(row >= s, pltpu.roll(x, s, axis=0), fill)


def _lru_kernel(x_ref, gb_ref, cinit_ref, h0_ref, cw_ref, cb_ref, wr_ref, br_ref, wi_ref, bi_ref,
                lam_ref, o_ref, hl_ref, xbuf, hcar, *, tt):
    d = x_ref.shape[-1]

    @pl.when(pl.program_id(1) == 0)
    def _():
        xbuf[0:SUBLANES, :] = cinit_ref[...]
        hcar[...] = h0_ref[...]

    xbuf[SUBLANES:SUBLANES + tt, :] = x_ref[...]
    base = SUBLANES - (CONV_W - 1)
    xc = cb_ref[...]
    for k in range(CONV_W):
        xc = xc + xbuf[pl.ds(base + k, tt), :] * cw_ref[k:k + 1, :]
    xbuf[0:SUBLANES, :] = xbuf[tt:tt + SUBLANES, :]

    xcb = xc.astype(_bf16)
    n_groups = d // MXU_DIM
    r_parts, i_parts = [], []
    for g in range(n_groups):
        xs = xcb[:, g * MXU_DIM:(g + 1) * MXU_DIM]
        r_parts.append(jnp.dot(xs, wr_ref[g], preferred_element_type=_f32))
        i_parts.append(jnp.dot(xs, wi_ref[g], preferred_element_type=_f32))
    r = jax.nn.sigmoid(jnp.concatenate(r_parts, axis=-1) + br_ref[...])
    gi = jax.nn.sigmoid(jnp.concatenate(i_parts, axis=-1) + bi_ref[...])
    z = -lam_ref[...]
    softplus = jnp.maximum(z, 0.0) + jnp.log(1.0 + jnp.exp(-jnp.abs(z)))
    log_a = (-LRU_C) * r * softplus
    a = jnp.exp(log_a)
    b = jnp.sqrt(1.0 - a * a) * (gi * xc)

    row = lax.broadcasted_iota(jnp.int32, (tt, d), 0) % SUBLANES
    s = 1
    while s < SUBLANES:
        b = a * _shift_rows(b, s, 0.0, row) + b
        a = a * _shift_rows(a, s, 1.0, row)
        s *= 2
    h_in = hcar[...]
    groups = []
    for g in range(tt // SUBLANES):
        rows = slice(g * SUBLANES, (g + 1) * SUBLANES)
        h_g = a[rows] * h_in + b[rows]
        groups.append(h_g)
        h_in = h_g[SUBLANES - 1:SUBLANES, :]
    h = jnp.concatenate(groups, axis=0)
    h_last = h[tt - 1:tt, :]
    hcar[...] = h_last
    hl_ref[...] = h_last
    o_ref[...] = (h * jax.nn.gelu(gb_ref[...].astype(_f32))).astype(o_ref.dtype)


def _lru(proj32, proj16, conv_init, h0, weights, *, n_batch, seq, tt, row0):
    d = conv_init.shape[-1]
    steps = seq // tt
    blk0 = row0 // tt
    (cw, cb, wr, br, wi, bi, lam) = weights
    const2 = lambda b, i: (0, 0)
    const3 = lambda b, i: (0, 0, 0)
    kern = functools.partial(_lru_kernel, tt=tt)
    return pl.pallas_call(
        kern,
        grid=(n_batch, steps),
        in_specs=[pl.BlockSpec((tt, d), lambda b, i: (blk0 + b * steps + i, XR_COL)),
                  pl.BlockSpec((tt, d), lambda b, i: (blk0 + b * steps + i, GB_COL)),
                  pl.BlockSpec((None, SUBLANES, d), lambda b, i: (b, 0, 0)),
                  pl.BlockSpec((None, 1, d), lambda b, i: (b, 0, 0)),
                  pl.BlockSpec((CONV_W, d), const2),
                  pl.BlockSpec((1, d), const2),
                  pl.BlockSpec(wr.shape, const3),
                  pl.BlockSpec((1, d), const2),
                  pl.BlockSpec(wi.shape, const3),
                  pl.BlockSpec((1, d), const2),
                  pl.BlockSpec((1, d), const2)],
        out_specs=[pl.BlockSpec((tt, d), lambda b, i: (b * steps + i, 0)),
                   pl.BlockSpec((None, 1, d), lambda b, i: (b, 0, 0))],
        out_shape=[jax.ShapeDtypeStruct((n_batch * seq, d), _bf16),
                   jax.ShapeDtypeStruct((n_batch, 1, d), _f32)],
        scratch_shapes=[pltpu.VMEM((tt + 2 * SUBLANES, d), _f32), pltpu.VMEM((1, d), _f32)],
        compiler_params=pltpu.CompilerParams(
            dimension_semantics=("parallel", "arbitrary"), vmem_limit_bytes=VMEM_LIMIT),
        name="conv_rglru",
    )(proj32, proj16, conv_init, h0, cw, cb, wr, br, wi, bi, lam)


def _merge_kernel(x_ref, oa_ref, ol_ref, ga_ref, gr_ref, wba_ref, wbl_ref, wo_ref, gn_ref, wq_ref,
                  x1_ref, xt_ref, qp_ref):
    ya = jnp.dot(oa_ref[...], wba_ref[...], preferred_element_type=_f32)
    yl = jnp.dot(ol_ref[...], wbl_ref[...], preferred_element_type=_f32)
    merged = (jax.nn.sigmoid(ga_ref[...].astype(_f32)) * ya
              + jax.nn.sigmoid(gr_ref[...].astype(_f32)) * yl)
    x1 = x_ref[...] + jnp.dot(merged.astype(_bf16), wo_ref[...], preferred_element_type=_f32)
    x1_ref[...] = x1
    xt = _rms(x1, gn_ref[...])
    xt_ref[...] = xt
    qp_ref[...] = jnp.dot(xt.astype(_bf16), wq_ref[...], preferred_element_type=_f32).astype(_bf16)


def _merge(x, o_attn, o_lru, proj16, w_ba, w_bl, w_o, g_ffn, wq):
    n, d = x.shape
    tm = _row_tile(n)
    nq = wq.shape[1]
    row = lambda i: (i, 0)
    const = lambda i: (0, 0)
    return pl.pallas_call(
        _merge_kernel,
        grid=(n // tm,),
        in_specs=[pl.BlockSpec((tm, d), row),
                  pl.BlockSpec((tm, d), row),
                  pl.BlockSpec((tm, d), row),
                  pl.BlockSpec((tm, d), lambda i: (i, GA_COL)),
                  pl.BlockSpec((tm, d), lambda i: (i, GR_COL)),
                  pl.BlockSpec((d, d), const, pipeline_mode=pl.Buffered(1)),
                  pl.BlockSpec((d, d), const, pipeline_mode=pl.Buffered(1)),
                  pl.BlockSpec((d, d), const, pipeline_mode=pl.Buffered(1)),
                  pl.BlockSpec((1, d), const),
                  pl.BlockSpec((d, nq), const, pipeline_mode=pl.Buffered(1))],
        out_specs=[pl.BlockSpec((tm, d), row), pl.BlockSpec((tm, d), row), pl.BlockSpec((tm, nq), row)],
        out_shape=[jax.ShapeDtypeStruct((n, d), _f32), jax.ShapeDtypeStruct((n, d), _f32),
                   jax.ShapeDtypeStruct((n, nq), _bf16)],
        compiler_params=pltpu.CompilerParams(
            dimension_semantics=("parallel",), vmem_limit_bytes=VMEM_LIMIT),
        name="merge_outproj_peerq",
    )(x, o_attn, o_lru, proj16, proj16, w_ba, w_bl, w_o, g_ffn, wq)


def _top16(s, payload=None):
    n_rows = s.shape[0]
    row = lax.broadcasted_iota(jnp.int32, s.shape, 0).astype(_f32)
    vals, picks = [], []
    for _ in range(TOPK):
        m = jnp.max(s, axis=0, keepdims=True)
        am = jnp.min(jnp.where(s == m, row, float(n_rows)), axis=0, keepdims=True)
        hit = row == am
        vals.append(m)
        if payload is None:
            picks.append(am)
        else:
            picks.append(jnp.max(jnp.where(hit, payload, -1.0), axis=0, keepdims=True))
        s = jnp.where(hit, -jnp.inf, s)
    return jnp.concatenate(vals, axis=0), jnp.concatenate(picks, axis=0)


def _candidates(a, b, mask_out):
    j8 = lax.broadcasted_iota(jnp.int32, (SUBLANES, a.shape[1]), 0)
    pieces = [a[0:1, :] + b]
    for i in range(1, SUBLANES):
        piece = a[i:i + 1, :] + b[0:SUBLANES, :]
        if mask_out:
            piece = jnp.where(j8 < TOPK // (i + 1), piece, -jnp.inf)
        pieces.append(piece)
    pieces.append(a[SUBLANES:TOPK, :] + b[0:1, :])
    return jnp.concatenate(pieces, axis=0)


def _topk_kernel(qp_ref, k1_ref, k2_ref, idx_ref, g_ref):
    nt = (((1,), (1,)), ((), ()))
    picks = []
    for h in range(PEER_HEADS):
        c0 = h * 2 * N_KEYS
        q1 = qp_ref[:, c0:c0 + N_KEYS]
        q2 = qp_ref[:, c0 + N_KEYS:c0 + 2 * N_KEYS]
        s1 = lax.dot_general(k1_ref[h], q1, nt, preferred_element_type=_f32)
        s2 = lax.dot_general(k2_ref[h], q2, nt, preferred_element_type=_f32)
        v1, i1 = _top16(s1)
        v2, i2 = _top16(s2)
        cand_s = _candidates(v1, v2, True)
        cand_i = _candidates(i1 * float(N_KEYS), i2, False)
        top_s, e_idx = _top16(cand_s, cand_i)
        p = jnp.exp(top_s - jnp.max(top_s, axis=0, keepdims=True))
        g = p / jnp.sum(p, axis=0, keepdims=True)
        picks.append(e_idx)
        g_ref[h * TOPK:(h + 1) * TOPK, :] = g
    idx_ref[...] = jnp.concatenate(picks, axis=0).T.astype(jnp.int32) * SLAB


def _topk(qp, keys1, keys2):
    n = qp.shape[0]
    nsel = PEER_HEADS * TOPK
    const3 = lambda i: (0, 0, 0)
    return pl.pallas_call(
        _topk_kernel,
        grid=(n // TOKEN_TILE,),
        in_specs=[pl.BlockSpec((TOKEN_TILE, qp.shape[1]), lambda i: (i, 0)),
                  pl.BlockSpec(keys1.shape, const3),
                  pl.BlockSpec(keys2.shape, const3)],
        out_specs=[pl.BlockSpec((TOKEN_TILE, nsel), lambda i: (i, 0)),
                   pl.BlockSpec((nsel, TOKEN_TILE), lambda i: (0, i))],
        out_shape=[jax.ShapeDtypeStruct((n, nsel), jnp.int32), jax.ShapeDtypeStruct((nsel, n), _f32)],
        compiler_params=pltpu.CompilerParams(
            dimension_semantics=("parallel",), vmem_limit_bytes=VMEM_LIMIT),
        name="peer_topk",
    )(qp, keys1, keys2)


def _pack_table(t):
    e, d = t.shape
    bits = lax.bitcast_convert_type(t.astype(_bf16), jnp.uint16).astype(jnp.uint32)
    words = bits[:, :d // 2] | (bits[:, d // 2:] << 16)
    return words.reshape(e * SLAB, LANES)


def _unpack(words):
    lo = pltpu.bitcast(words << 16, _f32)
    hi = pltpu.bitcast(words & jnp.uint32(0xFFFF0000), _f32)
    return lo, hi


def _gather_transposed(idx_ref, t, tab_ref, tile, slice_row):
    row = idx_ref.at[t] if slice_row else None
    for j in range(idx_ref.shape[1]):
        r = pl.multiple_of(row[j] if slice_row else idx_ref[t, j], SLAB)
        tile[pl.ds(j, SLAB, stride=SLAB_STRIDE), :] = tab_ref[pl.ds(r, SLAB), :]


def _tile_chunk(tile, s, nsel):
    return _unpack(tile[s * SLAB_STRIDE:s * SLAB_STRIDE + nsel, :])


def _split_bf16(x):
    hi = x.astype(_bf16)
    return hi, (x - hi.astype(_f32)).astype(_bf16)


def _mxu_right(x, rhs):
    hi, lo = _split_bf16(x)
    return (jnp.dot(hi, rhs, preferred_element_type=_f32) + jnp.dot(lo, rhs, preferred_element_type=_f32))


REDUCE_GROUP = 32


def _peer_u_kernel(idx_ref, xt_ref, g_ref, tab_ref, w_ref, tile_a, tile_b, parts):
    nsel = idx_ref.shape[1]

    def partial_dots(t, tile):
        x = xt_ref[t]
        acc = jnp.zeros((nsel, LANES), _f32)
        for s in range(SLAB):
            lo, hi = _tile_chunk(tile, s, nsel)
            acc = acc + lo * x[s:s + 1, :] + hi * x[SLAB + s:SLAB + s + 1, :]
        return acc

    tile_b[...] = jnp.zeros_like(tile_b)

    def two_tokens(i, carry):
        t = 2 * i
        t_prev = jnp.maximum(t - 1, 0)
        parts[t_prev] = partial_dots(t_prev, tile_b)
        _gather_transposed(idx_ref, t, tab_ref, tile_a, slice_row=False)
        _gather_transposed(idx_ref, t + 1, tab_ref, tile_b, slice_row=False)
        parts[t] = partial_dots(t, tile_a)
        return carry

    lax.fori_loop(0, TOKEN_TILE // 2, two_tokens, 0)
    parts[TOKEN_TILE - 1] = partial_dots(TOKEN_TILE - 1, tile_b)

    lane = lax.broadcasted_iota(jnp.int32, (nsel, TOKEN_TILE), 1)
    ones = jnp.ones((LANES, LANES), _bf16)

    def reduce_group(gi, act):
        for k in range(REDUCE_GROUP):
            t = gi * REDUCE_GROUP + k
            act = jnp.where(lane == t, _mxu_right(parts[t], ones), act)
        return act

    act = lax.fori_loop(0, TOKEN_TILE // REDUCE_GROUP, reduce_group, jnp.zeros((nsel, TOKEN_TILE), _f32))
    w_ref[...] = g_ref[...] * jax.nn.gelu(act)


def _peer_u(idx, xt, g_t, table):
    n, d = xt.shape
    nsel = idx.shape[1]
    xt = xt.reshape(n, d // LANES, LANES)
    return pl.pallas_call(
        _peer_u_kernel,
        grid=(n // TOKEN_TILE,),
        in_specs=[pl.BlockSpec((TOKEN_TILE, nsel), lambda i: (i, 0), memory_space=pltpu.SMEM),
                  pl.BlockSpec((TOKEN_TILE, d // LANES, LANES), lambda i: (i, 0, 0)),
                  pl.BlockSpec((nsel, TOKEN_TILE), lambda i: (0, i)),
                  _resident_table_spec(table)],
        out_specs=pl.BlockSpec((nsel, TOKEN_TILE), lambda i: (0, i)),
        out_shape=jax.ShapeDtypeStruct((nsel, n), _f32),
        scratch_shapes=[pltpu.VMEM((SLAB * SLAB_STRIDE, LANES), jnp.uint32),
                        pltpu.VMEM((SLAB * SLAB_STRIDE, LANES), jnp.uint32),
                        pltpu.VMEM((TOKEN_TILE, nsel, LANES), _f32)],
        compiler_params=pltpu.CompilerParams(
            dimension_semantics=("arbitrary",), vmem_limit_bytes=VMEM_LIMIT),
        name="peer_u_pass",
    )(idx, xt, g_t, table)


def _resident_table_spec(table):
    return pl.BlockSpec(table.shape, lambda i: (0, 0), pipeline_mode=pl.Buffered(1))


def _peer_v_kernel(idx_ref, w_ref, tab_ref, o_ref, tile_a, tile_b, wcol_b):
    nsel = idx_ref.shape[1]
    lane = lax.broadcasted_iota(jnp.int32, (nsel, TOKEN_TILE), 1)

    def weight_column(t):
        w_col = jnp.sum(jnp.where(lane == t, w_ref[...], 0.0), axis=1, keepdims=True)
        return jnp.broadcast_to(w_col, (nsel, LANES))

    def weighted_sum(t, tile, w_col):
        rows_lo, rows_hi = [], []
        for s in range(SLAB):
            lo, hi = _tile_chunk(tile, s, nsel)
            rows_lo.append(jnp.sum(lo * w_col, axis=0, keepdims=True))
            rows_hi.append(jnp.sum(hi * w_col, axis=0, keepdims=True))
        o_ref[t] = jnp.concatenate(rows_lo + rows_hi, axis=0)

    tile_b[...] = jnp.zeros_like(tile_b)
    wcol_b[...] = jnp.zeros_like(wcol_b)

    def two_tokens(i, carry):
        t = 2 * i
        w_a = weight_column(t)
        w_b = weight_column(t + 1)
        weighted_sum(jnp.maximum(t - 1, 0), tile_b, wcol_b[...])
        _gather_transposed(idx_ref, t, tab_ref, tile_a, slice_row=True)
        _gather_transposed(idx_ref, t + 1, tab_ref, tile_b, slice_row=True)
        weighted_sum(t, tile_a, w_a)
        wcol_b[...] = w_b
        return carry

    lax.fori_loop(0, TOKEN_TILE // 2, two_tokens, 0)
    weighted_sum(TOKEN_TILE - 1, tile_b, wcol_b[...])


def _peer_v(idx, w_t, table, d):
    n, nsel = idx.shape
    return pl.pallas_call(
        _peer_v_kernel,
        grid=(n // TOKEN_TILE,),
        in_specs=[pl.BlockSpec((TOKEN_TILE, nsel), lambda i: (i, 0), memory_space=pltpu.SMEM),
                  pl.BlockSpec((nsel, TOKEN_TILE), lambda i: (0, i)),
                  _resident_table_spec(table)],
        out_specs=pl.BlockSpec((TOKEN_TILE, d // LANES, LANES), lambda i: (i, 0, 0)),
        out_shape=jax.ShapeDtypeStruct((n, d // LANES, LANES), _f32),
        scratch_shapes=[pltpu.VMEM((SLAB * SLAB_STRIDE, LANES), jnp.uint32),
                        pltpu.VMEM((SLAB * SLAB_STRIDE, LANES), jnp.uint32),
                        pltpu.VMEM((nsel, LANES), _f32)],
        compiler_params=pltpu.CompilerParams(
            dimension_semantics=("arbitrary",), vmem_limit_bytes=VMEM_LIMIT),
        name="peer_v_pass",
    )(idx, w_t, table)


def _final_kernel(x1_ref, p_ref, g_ref, y_ref):
    y_ref[...] = _rms(x1_ref[...] + p_ref[...], g_ref[...])


def _final(x1, peer, g):
    n, d = x1.shape
    tm = _row_tile(n)
    row = lambda i: (i, 0)
    return pl.pallas_call(
        _final_kernel,
        grid=(n // tm,),
        in_specs=[pl.BlockSpec((tm, d), row), pl.BlockSpec((tm, d), row), pl.BlockSpec((1, d), lambda i: (0, 0))],
        out_specs=pl.BlockSpec((tm, d), row),
        out_shape=jax.ShapeDtypeStruct((n, d), _f32),
        compiler_params=pltpu.CompilerParams(
            dimension_semantics=("parallel",), vmem_limit_bytes=VMEM_LIMIT),
        name="residual_final_norm",
    )(x1, peer, g)


def _rel_bias(rel_table, n_q, n_k, offset):
    m = jnp.arange(n_k + n_q - 1)
    diag = rel_table[:, jnp.clip(offset + n_q - 1 - m, -MAX_REL, MAX_REL) + MAX_REL].astype(_f32)
    return jnp.stack([diag[:, n_q - 1 - q:n_q - 1 - q + n_k] for q in range(n_q)], axis=1)


def _block_diag_groups(w):
    nb, bi, bo = w.shape
    per = MXU_DIM // bi
    w = w.reshape(nb // per, per, bi, bo)
    eye = jnp.eye(per, dtype=w.dtype)
    return jnp.einsum("gpio,pq->gpiqo", w, eye).reshape(nb // per, per * bi, per * bo)


def _layer(xp, xs, cache_k, cache_v, state_conv, state_lru, norm_mix, w_in, rel_table, conv_w, conv_b,
           lru_wr, lru_br, lru_wi, lru_bi, lru_lambda, w_ba, w_bl, w_o, norm_ffn, peer_wq, keys1, keys2,
           tab_u, tab_v, norm_final):
    bsz, seq, d = xp.shape
    dbsz, dseq, _ = xs.shape
    n_p, n_s = bsz * seq, dbsz * dseq
    xp2, xs2 = xp.reshape(n_p, d), xs.reshape(n_s, d)
    row = lambda v: v.reshape(1, -1)

    w_in_b = w_in.astype(_bf16)
    proj, proj16 = _inproj(xp2, row(norm_mix), w_in_b)
    proj_s, proj16_s = _inproj(xs2, row(norm_mix), w_in_b)

    tiles = seq // BAND
    bias_p = _rel_bias(rel_table, CHUNK, BAND + CHUNK, BAND)
    prev = lambda c: (lambda b, i: (b * tiles + jnp.maximum(i - 1, 0), c))
    cur = lambda c: (lambda b, i: (b * tiles + i, c))
    o_attn_p = _attention(
        proj16, proj, proj, proj, bias_p, out_rows=n_p, n_chunks=BAND // CHUNK, cq=CHUNK, mask_first=True,
        grid=(bsz, tiles), q_map=cur(Q_COL), k_map=cur(K_COL), v_map=cur(V_COL),
        kprev_spec=pl.BlockSpec((BAND, d), prev(K_COL)), vprev_spec=pl.BlockSpec((BAND, d), prev(V_COL)))
    rows_c = cache_k.shape[1]
    bias_s = _rel_bias(rel_table, dseq, rows_c + dseq, rows_c)
    samp = lambda c: (lambda b, i: (b, c))
    cache_spec = pl.BlockSpec((None, rows_c, d), lambda b, i: (b, 0, 0))
    o_attn_s = _attention(
        proj16_s, proj_s, cache_k.reshape(dbsz, rows_c, d), cache_v.reshape(dbsz, rows_c, d), bias_s,
        out_rows=n_s, n_chunks=1, cq=dseq, mask_first=False, grid=(dbsz, 1),
        q_map=samp(Q_COL), k_map=samp(K_COL), v_map=samp(V_COL), kprev_spec=cache_spec, vprev_spec=cache_spec)

    lru_w = (conv_w, row(conv_b), _block_diag_groups(lru_wr).astype(_bf16), row(lru_br),
             _block_diag_groups(lru_wi).astype(_bf16), row(lru_bi), row(lru_lambda))
    pad_state = lambda st: jnp.pad(st, ((0, 0), (SUBLANES - (CONV_W - 1), 0), (0, 0)))
    o_lru_p, h_p = _lru(proj, proj16, jnp.zeros((bsz, SUBLANES, d), _f32), jnp.zeros((bsz, 1, d), _f32), lru_w,
                        n_batch=bsz, seq=seq, tt=256, row0=0)
    o_lru_s, h_s = _lru(proj_s, proj16_s, pad_state(state_conv), state_lru.reshape(dbsz, 1, d), lru_w,
                        n_batch=dbsz, seq=dseq, tt=dseq, row0=0)

    mix_w = (w_ba.astype(_bf16), w_bl.astype(_bf16), w_o.astype(_bf16), row(norm_ffn), peer_wq.astype(_bf16))
    k1, k2 = keys1.astype(_bf16), keys2.astype(_bf16)

    def ffn(x2, o_attn, o_lru, pr):
        x1, xt, qp = _merge(x2, o_attn, o_lru, pr, *mix_w)
        idx, g_t = _topk(qp, k1, k2)
        w_t = _peer_u(idx, xt, g_t, tab_u)
        peer = _peer_v(idx, w_t, tab_v, d).reshape(x2.shape)
        return _final(x1, peer, row(norm_final))

    y_p = ffn(xp2, o_attn_p, o_lru_p, proj16).reshape(bsz, seq, d)
    y_s = ffn(xs2, o_attn_s, o_lru_s, proj16_s).reshape(dbsz, dseq, d)

    def window(pr, b, t, rows, c):
        return pr.reshape(b, t, -1)[:, t - rows:, c * d:(c + 1) * d]

    heads = lambda a: a.reshape(a.shape[0], a.shape[1], N_HEADS, HEAD_DIM)
    rows_p = min(BAND, seq)
    keep = CONV_W - 1
    xr_s = jnp.concatenate([state_conv, window(proj_s, dbsz, dseq, dseq, XR_COL)], axis=1)
    return (y_p, y_s, heads(window(proj, bsz, seq, rows_p, K_COL)), heads(window(proj, bsz, seq, rows_p, V_COL)),
            window(proj, bsz, seq, keep, XR_COL), h_p.reshape(bsz, d),
            heads(window(proj_s, dbsz, dseq, dseq, K_COL)), heads(window(proj_s, dbsz, dseq, dseq, V_COL)),
            xr_s[:, xr_s.shape[1] - keep:], h_s.reshape(dbsz, d))


def kernel(x_prompt, x_sample, cache_k, cache_v, state_conv, state_lru, norm_mix, w_in, rel_table, conv_w, conv_b, lru_wr, lru_br, lru_wi, lru_bi, lru_lambda, w_branch_attn, w_branch_lru, w_out, norm_ffn, peer_wq, peer_keys1, peer_keys2, peer_u, peer_v, norm_final):
    depth = w_in.shape[0]
    assert depth == 1, "the final norm is fused into the single layer's last kernel"
    l = 0
    (y_prompt, y_sample, k_p, v_p, c_p, h_p, k_s, v_s, c_s, h_s) = _layer(
        x_prompt, x_sample, cache_k[l], cache_v[l], state_conv[l], state_lru[l], norm_mix[l], w_in[l],
        rel_table[l], conv_w[l], conv_b[l], lru_wr[l], lru_br[l], lru_wi[l], lru_bi[l], lru_lambda[l],
        w_branch_attn[l], w_branch_lru[l], w_out[l], norm_ffn[l], peer_wq[l], peer_keys1[l], peer_keys2[l],
        _pack_table(peer_u[l]), _pack_table(peer_v[l]), norm_final)
    stack = lambda v: v[None]
    return (y_prompt, y_sample, stack(k_p), stack(v_p), stack(c_p), stack(h_p),
            stack(k_s), stack(v_s), stack(c_s), stack(h_s))
```

```python
import functools

import jax
import jax.numpy as jnp
from jax import lax
from jax.experimental import pallas as pl
from jax.experimental.pallas import tpu as pltpu

CHUNK = 64
LEFT_CHUNKS = 8
BAND = LEFT_CHUNKS * CHUNK
N_HEADS = 16
HEAD_DIM = 64
ATTN_SCALE = HEAD_DIM ** -0.5
MAX_REL = 128
CONV_W = 4
LRU_C = 8.0
PEER_HEADS = 8
N_KEYS = 128
TOPK = 16
EPS = 1e-6
NEG_INF = -1e30

LANES = 128
SUBLANES = 8
MXU_DIM = 256
TOKEN_TILE = 128
SLAB = 4
SLAB_STRIDE = 136
VMEM_LIMIT = 56 * 1024 * 1024

_bf16 = jnp.bfloat16
_f32 = jnp.float32


def _row_tile(n_rows):
    for t in (512, 384, 256, 128):
        if n_rows % t == 0:
            return t
    raise ValueError(f"token count {n_rows} must be a multiple of 128")


def _rms(x, g):
    return x * lax.rsqrt(jnp.mean(x * x, axis=-1, keepdims=True) + EPS) * g


F32_BLOCKS = (1, 2, 3)
BF16_BLOCKS = (0, 4, 5, 6)
K_COL, V_COL, XR_COL = 0, 1, 2
Q_COL, GB_COL, GA_COL, GR_COL = 0, 1, 2, 3


def _inproj_kernel(x_ref, g_ref, w_ref, o32_ref, o16_ref):
    d = x_ref.shape[1]
    xn = _rms(x_ref[...], g_ref[...]).astype(_bf16)
    for j in range(w_ref.shape[1] // d):
        y = jnp.dot(xn, w_ref[:, j * d:(j + 1) * d], preferred_element_type=_f32)
        if j in F32_BLOCKS:
            c = F32_BLOCKS.index(j)
            o32_ref[:, c * d:(c + 1) * d] = y
        else:
            c = BF16_BLOCKS.index(j)
            o16_ref[:, c * d:(c + 1) * d] = y.astype(_bf16)


def _inproj(x, g, w):
    n, d = x.shape
    cols = w.shape[1]
    assert cols == (len(F32_BLOCKS) + len(BF16_BLOCKS)) * d
    tm = 256 if n % 256 == 0 else TOKEN_TILE
    n32, n16 = len(F32_BLOCKS) * d, len(BF16_BLOCKS) * d
    return pl.pallas_call(
        _inproj_kernel,
        grid=(n // tm,),
        in_specs=[pl.BlockSpec((tm, d), lambda i: (i, 0)),
                  pl.BlockSpec((1, d), lambda i: (0, 0)),
                  pl.BlockSpec((d, cols), lambda i: (0, 0), pipeline_mode=pl.Buffered(1))],
        out_specs=[pl.BlockSpec((tm, n32), lambda i: (i, 0)), pl.BlockSpec((tm, n16), lambda i: (i, 0))],
        out_shape=[jax.ShapeDtypeStruct((n, n32), _f32), jax.ShapeDtypeStruct((n, n16), _bf16)],
        compiler_params=pltpu.CompilerParams(
            dimension_semantics=("parallel",), vmem_limit_bytes=VMEM_LIMIT),
        name="inproj",
    )(x, g, w)


def _attn_kernel(q_ref, kp_ref, kc_ref, vp_ref, vc_ref, bias_ref, o_ref, kcat, vcat,
                 *, n_chunks, cq, mask_first):
    nq = n_chunks * cq
    span = BAND + cq
    kcat[0:BAND, :] = kp_ref[...].astype(_bf16)
    kcat[BAND:BAND + nq, :] = kc_ref[...].astype(_bf16)
    vcat[0:BAND, :] = vp_ref[...].astype(_bf16)
    vcat[BAND:BAND + nq, :] = vc_ref[...].astype(_bf16)
    first = pl.program_id(1) == 0
    pair = 2 * HEAD_DIM
    low_head = lax.broadcasted_iota(jnp.int32, (cq, pair), 1) < HEAD_DIM

    def chunk(cl, carry):
        r0 = pl.multiple_of(cl * cq, cq)
        q = q_ref[pl.ds(r0, cq), :] * ATTN_SCALE
        kw = kcat[pl.ds(r0, span), :]
        vw = vcat[pl.ds(r0, span), :]
        if mask_first:
            col = lax.broadcasted_iota(jnp.int32, (cq, span), 1)
            valid = jnp.logical_or(jnp.logical_not(first), col + r0 >= BAND)
        scores = []
        for pr in range(N_HEADS // 2):
            sl = slice(pr * pair, (pr + 1) * pair)
            q_pair = jnp.concatenate([jnp.where(low_head, q[:, sl], 0.0), jnp.where(low_head, 0.0, q[:, sl])],
                                     axis=0).astype(_bf16)
            s_pair = lax.dot_general(q_pair, kw[:, sl], (((1,), (1,)), ((), ())), preferred_element_type=_f32)
            scores += [s_pair[:cq], s_pair[cq:]]
        probs, inv = [], []
        for h in range(N_HEADS):
            s = scores[h] + bias_ref[h]
            if mask_first:
                s = jnp.where(valid, s, NEG_INF)
            p = jnp.exp(s - jnp.max(s, axis=-1, keepdims=True))
            inv.append(1.0 / jnp.sum(p, axis=-1, keepdims=True))
            probs.append(p.astype(_bf16))
        outs = []
        for pr in range(N_HEADS // 2):
            vp = vw[:, pr * pair:(pr + 1) * pair]
            o_pair = jnp.dot(jnp.concatenate([probs[2 * pr], probs[2 * pr + 1]], axis=0), vp,
                             preferred_element_type=_f32)
            outs.append(jnp.where(low_head, o_pair[:cq] * inv[2 * pr], o_pair[cq:] * inv[2 * pr + 1]))
        o_ref[pl.ds(r0, cq), :] = jnp.concatenate(outs, axis=-1).astype(o_ref.dtype)
        return carry

    lax.fori_loop(0, n_chunks, chunk, 0)


def _attention(q_src, kv_src, kprev_src, vprev_src, bias, *, out_rows, n_chunks, cq, mask_first, grid,
               q_map, k_map, v_map, kprev_spec, vprev_spec):
    nq = n_chunks * cq
    d = N_HEADS * HEAD_DIM
    span = BAND + cq
    kern = functools.partial(_attn_kernel, n_chunks=n_chunks, cq=cq, mask_first=mask_first)
    return pl.pallas_call(
        kern,
        grid=grid,
        in_specs=[pl.BlockSpec((nq, d), q_map),
                  kprev_spec,
                  pl.BlockSpec((nq, d), k_map),
                  vprev_spec,
                  pl.BlockSpec((nq, d), v_map),
                  pl.BlockSpec((N_HEADS, cq, span), lambda b, i: (0, 0, 0))],
        out_specs=pl.BlockSpec((nq, d), lambda b, i, g=grid: (b * g[1] + i, 0)),
        out_shape=jax.ShapeDtypeStruct((out_rows, d), _bf16),
        scratch_shapes=[pltpu.VMEM((BAND + nq, d), _bf16), pltpu.VMEM((BAND + nq, d), _bf16)],
        compiler_params=pltpu.CompilerParams(
            dimension_semantics=("parallel", "arbitrary"), vmem_limit_bytes=VMEM_LIMIT),
        name="band_attention",
    )(q_src, kprev_src, kv_src, vprev_src, kv_src, bias)


def _shift_rows(x, s, fill, row):
    return jnp.where(row >= s, pltpu.roll(x, s, axis=0), fill)


def _lru_kernel(x_ref, gb_ref, cinit_ref, h0_ref, cw_ref, cb_ref, wr_ref, br_ref, wi_ref, bi_ref,
                lam_ref, o_ref, hl_ref, xbuf, hcar, *, tt):
    d = x_ref.shape[-1]

    @pl.when(pl.program_id(1) == 0)
    def _():
        xbuf[0:SUBLANES, :] = cinit_ref[...]
        hcar[...] = h0_ref[...]

    xbuf[SUBLANES:SUBLANES + tt, :] = x_ref[...]
    base = SUBLANES - (CONV_W - 1)
    xc = cb_ref[...]
    for k in range(CONV_W):
        xc = xc + xbuf[pl.ds(base + k, tt), :] * cw_ref[k:k + 1, :]
    xbuf[0:SUBLANES, :] = xbuf[tt:tt + SUBLANES, :]

    xcb = xc.astype(_bf16)
    n_groups = d // MXU_DIM
    r_parts, i_parts = [], []
    for g in range(n_groups):
        xs = xcb[:, g * MXU_DIM:(g + 1) * MXU_DIM]
        r_parts.append(jnp.dot(xs, wr_ref[g], preferred_element_type=_f32))
        i_parts.append(jnp.dot(xs, wi_ref[g], preferred_element_type=_f32))
    r = jax.nn.sigmoid(jnp.concatenate(r_parts, axis=-1) + br_ref[...])
    gi = jax.nn.sigmoid(jnp.concatenate(i_parts, axis=-1) + bi_ref[...])
    z = -lam_ref[...]
    softplus = jnp.maximum(z, 0.0) + jnp.log(1.0 + jnp.exp(-jnp.abs(z)))
    log_a = (-LRU_C) * r * softplus
    a = jnp.exp(log_a)
    b = jnp.sqrt(1.0 - a * a) * (gi * xc)

    row = lax.broadcasted_iota(jnp.int32, (tt, d), 0) % SUBLANES
    s = 1
    while s < SUBLANES:
        b = a * _shift_rows(b, s, 0.0, row) + b
        a = a * _shift_rows(a, s, 1.0, row)
        s *= 2
    h_in = hcar[...]
    groups = []
    for g in range(tt // SUBLANES):
        rows = slice(g * SUBLANES, (g + 1) * SUBLANES)
        h_g = a[rows] * h_in + b[rows]
        groups.append(h_g)
        h_in = h_g[SUBLANES - 1:SUBLANES, :]
    h = jnp.concatenate(groups, axis=0)
    h_last = h[tt - 1:tt, :]
    hcar[...] = h_last
    hl_ref[...] = h_last
    o_ref[...] = (h * jax.nn.gelu(gb_ref[...].astype(_f32))).astype(o_ref.dtype)


def _lru(proj32, proj16, conv_init, h0, weights, *, n_batch, seq, tt, row0):
    d = conv_init.shape[-1]
    steps = seq // tt
    blk0 = row0 // tt
    (cw, cb, wr, br, wi, bi, lam) = weights
    const2 = lambda b, i: (0, 0)
    const3 = lambda b, i: (0, 0, 0)
    kern = functools.partial(_lru_kernel, tt=tt)
    return pl.pallas_call(
        kern,
        grid=(n_batch, steps),
        in_specs=[pl.BlockSpec((tt, d), lambda b, i: (blk0 + b * steps + i, XR_COL)),
                  pl.BlockSpec((tt, d), lambda b, i: (blk0 + b * steps + i, GB_COL)),
                  pl.BlockSpec((None, SUBLANES, d), lambda b, i: (b, 0, 0)),
                  pl.BlockSpec((None, 1, d), lambda b, i: (b, 0, 0)),
                  pl.BlockSpec((CONV_W, d), const2),
                  pl.BlockSpec((1, d), const2),
                  pl.BlockSpec(wr.shape, const3),
                  pl.BlockSpec((1, d), const2),
                  pl.BlockSpec(wi.shape, const3),
                  pl.BlockSpec((1, d), const2),
                  pl.BlockSpec((1, d), const2)],
        out_specs=[pl.BlockSpec((tt, d), lambda b, i: (b * steps + i, 0)),
                   pl.BlockSpec((None, 1, d), lambda b, i: (b, 0, 0))],
        out_shape=[jax.ShapeDtypeStruct((n_batch * seq, d), _bf16),
                   jax.ShapeDtypeStruct((n_batch, 1, d), _f32)],
        scratch_shapes=[pltpu.VMEM((tt + 2 * SUBLANES, d), _f32), pltpu.VMEM((1, d), _f32)],
        compiler_params=pltpu.CompilerParams(
            dimension_semantics=("parallel", "arbitrary"), vmem_limit_bytes=VMEM_LIMIT),
        name="conv_rglru",
    )(proj32, proj16, conv_init, h0, cw, cb, wr, br, wi, bi, lam)


def _merge_kernel(x_ref, oa_ref, ol_ref, ga_ref, gr_ref, wba_ref, wbl_ref, wo_ref, gn_ref, wq_ref,
                  x1_ref, xt_ref, qp_ref):
    ya = jnp.dot(oa_ref[...], wba_ref[...], preferred_element_type=_f32)
    yl = jnp.dot(ol_ref[...], wbl_ref[...], preferred_element_type=_f32)
    merged = (jax.nn.sigmoid(ga_ref[...].astype(_f32)) * ya
              + jax.nn.sigmoid(gr_ref[...].astype(_f32)) * yl)
    x1 = x_ref[...] + jnp.dot(merged.astype(_bf16), wo_ref[...], preferred_element_type=_f32)
    x1_ref[...] = x1
    xt = _rms(x1, gn_ref[...])
    xt_ref[...] = xt.reshape(xt_ref.shape)
    qp_ref[...] = jnp.dot(xt.astype(_bf16), wq_ref[...], preferred_element_type=_f32).astype(_bf16)


def _merge(x, o_attn, o_lru, proj16, w_ba, w_bl, w_o, g_ffn, wq):
    n, d = x.shape
    tm = _row_tile(n)
    nq = wq.shape[1]
    row = lambda i: (i, 0)
    const = lambda i: (0, 0)
    return pl.pallas_call(
        _merge_kernel,
        grid=(n // tm,),
        in_specs=[pl.BlockSpec((tm, d), row),
                  pl.BlockSpec((tm, d), row),
                  pl.BlockSpec((tm, d), row),
                  pl.BlockSpec((tm, d), lambda i: (i, GA_COL)),
                  pl.BlockSpec((tm, d), lambda i: (i, GR_COL)),
                  pl.BlockSpec((d, d), const, pipeline_mode=pl.Buffered(1)),
                  pl.BlockSpec((d, d), const, pipeline_mode=pl.Buffered(1)),
                  pl.BlockSpec((d, d), const, pipeline_mode=pl.Buffered(1)),
                  pl.BlockSpec((1, d), const),
                  pl.BlockSpec((d, nq), const, pipeline_mode=pl.Buffered(1))],
        out_specs=[pl.BlockSpec((tm, d), row), pl.BlockSpec((tm, d // LANES, LANES), lambda i: (i, 0, 0)),
                   pl.BlockSpec((tm, nq), row)],
        out_shape=[jax.ShapeDtypeStruct((n, d), _f32), jax.ShapeDtypeStruct((n, d // LANES, LANES), _f32),
                   jax.ShapeDtypeStruct((n, nq), _bf16)],
        compiler_params=pltpu.CompilerParams(
            dimension_semantics=("parallel",), vmem_limit_bytes=VMEM_LIMIT),
        name="merge_outproj_peerq",
    )(x, o_attn, o_lru, proj16, proj16, w_ba, w_bl, w_o, g_ffn, wq)


def _top16(s, payload=None):
    n_rows = s.shape[0]
    row = lax.broadcasted_iota(jnp.int32, s.shape, 0).astype(_f32)
    vals, picks = [], []
    for _ in range(TOPK):
        m = jnp.max(s, axis=0, keepdims=True)
        am = jnp.min(jnp.where(s == m, row, float(n_rows)), axis=0, keepdims=True)
        hit = row == am
        vals.append(m)
        if payload is None:
            picks.append(am)
        else:
            picks.append(jnp.max(jnp.where(hit, payload, -1.0), axis=0, keepdims=True))
        s = jnp.where(hit, -jnp.inf, s)
    return jnp.concatenate(vals, axis=0), jnp.concatenate(picks, axis=0)


def _candidates(a, b, mask_out):
    j8 = lax.broadcasted_iota(jnp.int32, (SUBLANES, a.shape[1]), 0)
    pieces = [a[0:1, :] + b]
    for i in range(1, SUBLANES):
        piece = a[i:i + 1, :] + b[0:SUBLANES, :]
        if mask_out:
            piece = jnp.where(j8 < TOPK // (i + 1), piece, -jnp.inf)
        pieces.append(piece)
    pieces.append(a[SUBLANES:TOPK, :] + b[0:1, :])
    return jnp.concatenate(pieces, axis=0)


def _topk_kernel(qp_ref, k1_ref, k2_ref, idx_ref, g_ref):
    nt = (((1,), (1,)), ((), ()))
    picks = []
    for h in range(PEER_HEADS):
        c0 = h * 2 * N_KEYS
        q1 = qp_ref[:, c0:c0 + N_KEYS]
        q2 = qp_ref[:, c0 + N_KEYS:c0 + 2 * N_KEYS]
        s1 = lax.dot_general(k1_ref[h], q1, nt, preferred_element_type=_f32)
        s2 = lax.dot_general(k2_ref[h], q2, nt, preferred_element_type=_f32)
        v1, i1 = _top16(s1)
        v2, i2 = _top16(s2)
        cand_s = _candidates(v1, v2, True)
        cand_i = _candidates(i1 * float(N_KEYS), i2, False)
        top_s, e_idx = _top16(cand_s, cand_i)
        p = jnp.exp(top_s - jnp.max(top_s, axis=0, keepdims=True))
        g = p / jnp.sum(p, axis=0, keepdims=True)
        picks.append(e_idx)
        g_ref[h * TOPK:(h + 1) * TOPK, :] = g
    idx_ref[...] = jnp.concatenate(picks, axis=0).T.astype(jnp.int32) * SLAB


def _topk(qp, keys1, keys2):
    n = qp.shape[0]
    nsel = PEER_HEADS * TOPK
    const3 = lambda i: (0, 0, 0)
    return pl.pallas_call(
        _topk_kernel,
        grid=(n // TOKEN_TILE,),
        in_specs=[pl.BlockSpec((TOKEN_TILE, qp.shape[1]), lambda i: (i, 0)),
                  pl.BlockSpec(keys1.shape, const3),
                  pl.BlockSpec(keys2.shape, const3)],
        out_specs=[pl.BlockSpec((TOKEN_TILE, nsel), lambda i: (i, 0)),
                   pl.BlockSpec((nsel, TOKEN_TILE), lambda i: (0, i))],
        out_shape=[jax.ShapeDtypeStruct((n, nsel), jnp.int32), jax.ShapeDtypeStruct((nsel, n), _f32)],
        compiler_params=pltpu.CompilerParams(
            dimension_semantics=("parallel",), vmem_limit_bytes=VMEM_LIMIT),
        name="peer_topk",
    )(qp, keys1, keys2)


def _pack_table(t):
    e, d = t.shape
    bits = lax.bitcast_convert_type(t.astype(_bf16), jnp.uint16).astype(jnp.uint32)
    words = bits[:, :d // 2] | (bits[:, d // 2:] << 16)
    return words.reshape(e * SLAB, LANES)


def _unpack(words):
    lo = pltpu.bitcast(words << 16, _f32)
    hi = pltpu.bitcast(words & jnp.uint32(0xFFFF0000), _f32)
    return lo, hi


def _gather_transposed(idx_ref, t, tab_ref, tile, slice_row):
    row = idx_ref.at[t] if slice_row else None
    for j in range(idx_ref.shape[1]):
        r = pl.multiple_of(row[j] if slice_row else idx_ref[t, j], SLAB)
        tile[pl.ds(j, SLAB, stride=SLAB_STRIDE), :] = tab_ref[pl.ds(r, SLAB), :]


def _tile_chunk(tile, s, nsel):
    return _unpack(tile[s * SLAB_STRIDE:s * SLAB_STRIDE + nsel, :])


def _split_bf16(x):
    hi = x.astype(_bf16)
    return hi, (x - hi.astype(_f32)).astype(_bf16)


def _mxu_right(x, rhs):
    hi, lo = _split_bf16(x)
    return (jnp.dot(hi, rhs, preferred_element_type=_f32) + jnp.dot(lo, rhs, preferred_element_type=_f32))


REDUCE_GROUP = 32
U_TOKENS_PER_STEP = 2


def _peer_u_kernel(idx_ref, xt_ref, g_ref, tab_ref, w_ref, *scratch):
    tiles, parts = scratch[:-1], scratch[-1]
    nsel = idx_ref.shape[1]

    def partial_dots(t, tile):
        x = xt_ref[t]
        acc = jnp.zeros((nsel, LANES), _f32)
        for s in range(SLAB):
            lo, hi = _tile_chunk(tile, s, nsel)
            acc = acc + lo * x[s:s + 1, :] + hi * x[SLAB + s:SLAB + s + 1, :]
        return acc

    n_t = len(tiles)
    last = tiles[-1]
    last[...] = jnp.zeros_like(last)

    def step(i, carry):
        t = n_t * i
        t_prev = jnp.maximum(t - 1, 0)
        parts[t_prev] = partial_dots(t_prev, last)
        for k, tile in enumerate(tiles):
            _gather_transposed(idx_ref, t + k, tab_ref, tile, slice_row=False)
        for k, tile in enumerate(tiles[:-1]):
            parts[t + k] = partial_dots(t + k, tile)
        return carry

    lax.fori_loop(0, TOKEN_TILE // n_t, step, 0)
    parts[TOKEN_TILE - 1] = partial_dots(TOKEN_TILE - 1, last)

    lane = lax.broadcasted_iota(jnp.int32, (nsel, TOKEN_TILE), 1)
    ones = jnp.ones((LANES, LANES), _bf16)

    def reduce_group(gi, act):
        for k in range(REDUCE_GROUP):
            t = gi * REDUCE_GROUP + k
            act = jnp.where(lane == t, _mxu_right(parts[t], ones), act)
        return act

    act = lax.fori_loop(0, TOKEN_TILE // REDUCE_GROUP, reduce_group, jnp.zeros((nsel, TOKEN_TILE), _f32))
    w_ref[...] = g_ref[...] * jax.nn.gelu(act)


def _peer_u(idx, xt, g_t, table):
    n = xt.shape[0]
    d = xt.shape[1] * xt.shape[2]
    nsel = idx.shape[1]
    return pl.pallas_call(
        _peer_u_kernel,
        grid=(n // TOKEN_TILE,),
        in_specs=[pl.BlockSpec((TOKEN_TILE, nsel), lambda i: (i, 0), memory_space=pltpu.SMEM),
                  pl.BlockSpec((TOKEN_TILE, d // LANES, LANES), lambda i: (i, 0, 0)),
                  pl.BlockSpec((nsel, TOKEN_TILE), lambda i: (0, i)),
                  _resident_table_spec(table)],
        out_specs=pl.BlockSpec((nsel, TOKEN_TILE), lambda i: (0, i)),
        out_shape=jax.ShapeDtypeStruct((nsel, n), _f32),
        scratch_shapes=[pltpu.VMEM((SLAB * SLAB_STRIDE, LANES), jnp.uint32)] * U_TOKENS_PER_STEP
                       + [pltpu.VMEM((TOKEN_TILE, nsel, LANES), _f32)],
        compiler_params=pltpu.CompilerParams(
            dimension_semantics=("arbitrary",), vmem_limit_bytes=VMEM_LIMIT),
        name="peer_u_pass",
    )(idx, xt, g_t, table)


def _resident_table_spec(table):
    return pl.BlockSpec(table.shape, lambda i: (0, 0), pipeline_mode=pl.Buffered(1))


def _peer_v_kernel(idx_ref, w_ref, tab_ref, o_ref, tile_a, tile_b, wcol_b):
    nsel = idx_ref.shape[1]
    lane = lax.broadcasted_iota(jnp.int32, (nsel, TOKEN_TILE), 1)

    def weight_column(t):
        w_col = jnp.sum(jnp.where(lane == t, w_ref[...], 0.0), axis=1, keepdims=True)
        return jnp.broadcast_to(w_col, (nsel, LANES))

    def weighted_sum(t, tile, w_col):
        rows_lo, rows_hi = [], []
        for s in range(SLAB):
            lo, hi = _tile_chunk(tile, s, nsel)
            rows_lo.append(jnp.sum(lo * w_col, axis=0, keepdims=True))
            rows_hi.append(jnp.sum(hi * w_col, axis=0, keepdims=True))
        o_ref[t] = jnp.concatenate(rows_lo + rows_hi, axis=0)

    tile_b[...] = jnp.zeros_like(tile_b)
    wcol_b[...] = jnp.zeros_like(wcol_b)

    def two_tokens(i, carry):
        t = 2 * i
        w_a = weight_column(t)
        w_b = weight_column(t + 1)
        weighted_sum(jnp.maximum(t - 1, 0), tile_b, wcol_b[...])
        _gather_transposed(idx_ref, t, tab_ref, tile_a, slice_row=True)
        _gather_transposed(idx_ref, t + 1, tab_ref, tile_b, slice_row=True)
        weighted_sum(t, tile_a, w_a)
        wcol_b[...] = w_b
        return carry

    lax.fori_loop(0, TOKEN_TILE // 2, two_tokens, 0)
    weighted_sum(TOKEN_TILE - 1, tile_b, wcol_b[...])


def _peer_v(idx, w_t, table, d):
    n, nsel = idx.shape
    return pl.pallas_call(
        _peer_v_kernel,
        grid=(n // TOKEN_TILE,),
        in_specs=[pl.BlockSpec((TOKEN_TILE, nsel), lambda i: (i, 0), memory_space=pltpu.SMEM),
                  pl.BlockSpec((nsel, TOKEN_TILE), lambda i: (0, i)),
                  _resident_table_spec(table)],
        out_specs=pl.BlockSpec((TOKEN_TILE, d // LANES, LANES), lambda i: (i, 0, 0)),
        out_shape=jax.ShapeDtypeStruct((n, d // LANES, LANES), _f32),
        scratch_shapes=[pltpu.VMEM((SLAB * SLAB_STRIDE, LANES), jnp.uint32),
                        pltpu.VMEM((SLAB * SLAB_STRIDE, LANES), jnp.uint32),
                        pltpu.VMEM((nsel, LANES), _f32)],
        compiler_params=pltpu.CompilerParams(
            dimension_semantics=("arbitrary",), vmem_limit_bytes=VMEM_LIMIT),
        name="peer_v_pass",
    )(idx, w_t, table)


def _final_kernel(x1_ref, p_ref, g_ref, y_ref):
    y_ref[...] = _rms(x1_ref[...] + p_ref[...].reshape(x1_ref.shape), g_ref[...])


def _final(x1, peer, g):
    n, d = x1.shape
    tm = _row_tile(n)
    row = lambda i: (i, 0)
    return pl.pallas_call(
        _final_kernel,
        grid=(n // tm,),
        in_specs=[pl.BlockSpec((tm, d), row), pl.BlockSpec((tm, d // LANES, LANES), lambda i: (i, 0, 0)),
                  pl.BlockSpec((1, d), lambda i: (0, 0))],
        out_specs=pl.BlockSpec((tm, d), row),
        out_shape=jax.ShapeDtypeStruct((n, d), _f32),
        compiler_params=pltpu.CompilerParams(
            dimension_semantics=("parallel",), vmem_limit_bytes=VMEM_LIMIT),
        name="residual_final_norm",
    )(x1, peer, g)


def _rel_bias(rel_table, n_q, n_k, offset):
    m = jnp.arange(n_k + n_q - 1)
    diag = rel_table[:, jnp.clip(offset + n_q - 1 - m, -MAX_REL, MAX_REL) + MAX_REL].astype(_f32)
    return jnp.stack([diag[:, n_q - 1 - q:n_q - 1 - q + n_k] for q in range(n_q)], axis=1)


def _block_diag_groups(w):
    nb, bi, bo = w.shape
    per = MXU_DIM // bi
    w = w.reshape(nb // per, per, bi, bo)
    eye = jnp.eye(per, dtype=w.dtype)
    return jnp.einsum("gpio,pq->gpiqo", w, eye).reshape(nb // per, per * bi, per * bo)


def _layer(xp, xs, cache_k, cache_v, state_conv, state_lru, norm_mix, w_in, rel_table, conv_w, conv_b,
           lru_wr, lru_br, lru_wi, lru_bi, lru_lambda, w_ba, w_bl, w_o, norm_ffn, peer_wq, keys1, keys2,
           tab_u, tab_v, norm_final):
    bsz, seq, d = xp.shape
    dbsz, dseq, _ = xs.shape
    n_p, n_s = bsz * seq, dbsz * dseq
    xp2, xs2 = xp.reshape(n_p, d), xs.reshape(n_s, d)
    row = lambda v: v.reshape(1, -1)

    w_in_b = w_in.astype(_bf16)
    proj, proj16 = _inproj(xp2, row(norm_mix), w_in_b)
    proj_s, proj16_s = _inproj(xs2, row(norm_mix), w_in_b)

    tiles = seq // BAND
    bias_p = _rel_bias(rel_table, CHUNK, BAND + CHUNK, BAND)
    prev = lambda c: (lambda b, i: (b * tiles + jnp.maximum(i - 1, 0), c))
    cur = lambda c: (lambda b, i: (b * tiles + i, c))
    o_attn_p = _attention(
        proj16, proj, proj, proj, bias_p, out_rows=n_p, n_chunks=BAND // CHUNK, cq=CHUNK, mask_first=True,
        grid=(bsz, tiles), q_map=cur(Q_COL), k_map=cur(K_COL), v_map=cur(V_COL),
        kprev_spec=pl.BlockSpec((BAND, d), prev(K_COL)), vprev_spec=pl.BlockSpec((BAND, d), prev(V_COL)))
    rows_c = cache_k.shape[1]
    bias_s = _rel_bias(rel_table, dseq, rows_c + dseq, rows_c)
    samp = lambda c: (lambda b, i: (b, c))
    cache_spec = pl.BlockSpec((None, rows_c, d), lambda b, i: (b, 0, 0))
    o_attn_s = _attention(
        proj16_s, proj_s, cache_k.reshape(dbsz, rows_c, d), cache_v.reshape(dbsz, rows_c, d), bias_s,
        out_rows=n_s, n_chunks=1, cq=dseq, mask_first=False, grid=(dbsz, 1),
        q_map=samp(Q_COL), k_map=samp(K_COL), v_map=samp(V_COL), kprev_spec=cache_spec, vprev_spec=cache_spec)

    lru_w = (conv_w, row(conv_b), _block_diag_groups(lru_wr).astype(_bf16), row(lru_br),
             _block_diag_groups(lru_wi).astype(_bf16), row(lru_bi), row(lru_lambda))
    pad_state = lambda st: jnp.pad(st, ((0, 0), (SUBLANES - (CONV_W - 1), 0), (0, 0)))
    o_lru_p, h_p = _lru(proj, proj16, jnp.zeros((bsz, SUBLANES, d), _f32), jnp.zeros((bsz, 1, d), _f32), lru_w,
                        n_batch=bsz, seq=seq, tt=256, row0=0)
    o_lru_s, h_s = _lru(proj_s, proj16_s, pad_state(state_conv), state_lru.reshape(dbsz, 1, d), lru_w,
                        n_batch=dbsz, seq=dseq, tt=dseq, row0=0)

    mix_w = (w_ba.astype(_bf16), w_bl.astype(_bf16), w_o.astype(_bf16), row(norm_ffn), peer_wq.astype(_bf16))
    k1, k2 = keys1.astype(_bf16), keys2.astype(_bf16)

    def ffn(x2, o_attn, o_lru, pr):
        x1, xt, qp = _merge(x2, o_attn, o_lru, pr, *mix_w)
        idx, g_t = _topk(qp, k1, k2)
        w_t = _peer_u(idx, xt, g_t, tab_u)
        peer = _peer_v(idx, w_t, tab_v, d)
        return _final(x1, peer, row(norm_final))

    y_p = ffn(xp2, o_attn_p, o_lru_p, proj16).reshape(bsz, seq, d)
    y_s = ffn(xs2, o_attn_s, o_lru_s, proj16_s).reshape(dbsz, dseq, d)

    def window(pr, b, t, rows, c):
        return pr.reshape(b, t, -1)[:, t - rows:, c * d:(c + 1) * d]

    heads = lambda a: a.reshape(a.shape[0], a.shape[1], N_HEADS, HEAD_DIM)
    rows_p = min(BAND, seq)
    keep = CONV_W - 1
    xr_s = jnp.concatenate([state_conv, window(proj_s, dbsz, dseq, dseq, XR_COL)], axis=1)
    return (y_p, y_s, heads(window(proj, bsz, seq, rows_p, K_COL)), heads(window(proj, bsz, seq, rows_p, V_COL)),
            window(proj, bsz, seq, keep, XR_COL), h_p.reshape(bsz, d),
            heads(window(proj_s, dbsz, dseq, dseq, K_COL)), heads(window(proj_s, dbsz, dseq, dseq, V_COL)),
            xr_s[:, xr_s.shape[1] - keep:], h_s.reshape(dbsz, d))


def kernel(x_prompt, x_sample, cache_k, cache_v, state_conv, state_lru, norm_mix, w_in, rel_table, conv_w, conv_b, lru_wr, lru_br, lru_wi, lru_bi, lru_lambda, w_branch_attn, w_branch_lru, w_out, norm_ffn, peer_wq, peer_keys1, peer_keys2, peer_u, peer_v, norm_final):
    depth = w_in.shape[0]
    assert depth == 1, "the final norm is fused into the single layer's last kernel"
    l = 0
    (y_prompt, y_sample, k_p, v_p, c_p, h_p, k_s, v_s, c_s, h_s) = _layer(
        x_prompt, x_sample, cache_k[l], cache_v[l], state_conv[l], state_lru[l], norm_mix[l], w_in[l],
        rel_table[l], conv_w[l], conv_b[l], lru_wr[l], lru_br[l], lru_wi[l], lru_bi[l], lru_lambda[l],
        w_branch_attn[l], w_branch_lru[l], w_out[l], norm_ffn[l], peer_wq[l], peer_keys1[l], peer_keys2[l],
        _pack_table(peer_u[l]), _pack_table(peer_v[l]), norm_final)
    stack = lambda v: v[None]
    return (y_prompt, y_sample, stack(k_p), stack(v_p), stack(c_p), stack(h_p),
            stack(k_s), stack(v_s), stack(c_s), stack(h_s))
```

```python
import functools

import jax
import jax.numpy as jnp
from jax import lax
from jax.experimental import pallas as pl
from jax.experimental.pallas import tpu as pltpu

CHUNK = 64
LEFT_CHUNKS = 8
BAND = LEFT_CHUNKS * CHUNK
N_HEADS = 16
HEAD_DIM = 64
ATTN_SCALE = HEAD_DIM ** -0.5
MAX_REL = 128
CONV_W = 4
LRU_C = 8.0
PEER_HEADS = 8
N_KEYS = 128
TOPK = 16
EPS = 1e-6
NEG_INF = -1e30

LANES = 128
SUBLANES = 8
MXU_DIM = 256
TOKEN_TILE = 128
SLAB = 4
SLAB_STRIDE = 136
VMEM_LIMIT = 56 * 1024 * 1024

_bf16 = jnp.bfloat16
_f32 = jnp.float32


def _row_tile(n_rows):
    for t in (512, 384, 256, 128):
        if n_rows % t == 0:
            return t
    raise ValueError(f"token count {n_rows} must be a multiple of 128")


def _rms(x, g):
    return x * lax.rsqrt(jnp.mean(x * x, axis=-1, keepdims=True) + EPS) * g


F32_BLOCKS = (1, 2, 3)
BF16_BLOCKS = (0, 4, 5, 6)
K_COL, V_COL, XR_COL = 0, 1, 2
Q_COL, GB_COL, GA_COL, GR_COL = 0, 1, 2, 3


def _inproj_kernel(x_ref, g_ref, w_ref, o32_ref, o16_ref):
    d = x_ref.shape[1]
    xn = _rms(x_ref[...], g_ref[...]).astype(_bf16)
    for j in range(w_ref.shape[1] // d):
        y = jnp.dot(xn, w_ref[:, j * d:(j + 1) * d], preferred_element_type=_f32)
        if j in F32_BLOCKS:
            c = F32_BLOCKS.index(j)
            o32_ref[:, c * d:(c + 1) * d] = y
        else:
            c = BF16_BLOCKS.index(j)
            o16_ref[:, c * d:(c + 1) * d] = y.astype(_bf16)


def _inproj(x, g, w):
    n, d = x.shape
    cols = w.shape[1]
    assert cols == (len(F32_BLOCKS) + len(BF16_BLOCKS)) * d
    tm = 256 if n % 256 == 0 else TOKEN_TILE
    n32, n16 = len(F32_BLOCKS) * d, len(BF16_BLOCKS) * d
    return pl.pallas_call(
        _inproj_kernel,
        grid=(n // tm,),
        in_specs=[pl.BlockSpec((tm, d), lambda i: (i, 0)),
                  pl.BlockSpec((1, d), lambda i: (0, 0)),
                  pl.BlockSpec((d, cols), lambda i: (0, 0), pipeline_mode=pl.Buffered(1))],
        out_specs=[pl.BlockSpec((tm, n32), lambda i: (i, 0)), pl.BlockSpec((tm, n16), lambda i: (i, 0))],
        out_shape=[jax.ShapeDtypeStruct((n, n32), _f32), jax.ShapeDtypeStruct((n, n16), _bf16)],
        compiler_params=pltpu.CompilerParams(
            dimension_semantics=("parallel",), vmem_limit_bytes=VMEM_LIMIT),
        name="inproj",
    )(x, g, w)


def _attn_kernel(q_ref, kp_ref, kc_ref, vp_ref, vc_ref, bias_ref, o_ref, kcat, vcat,
                 *, n_chunks, cq, mask_first):
    nq = n_chunks * cq
    span = BAND + cq
    kcat[0:BAND, :] = kp_ref[...].astype(_bf16)
    kcat[BAND:BAND + nq, :] = kc_ref[...].astype(_bf16)
    vcat[0:BAND, :] = vp_ref[...].astype(_bf16)
    vcat[BAND:BAND + nq, :] = vc_ref[...].astype(_bf16)
    first = pl.program_id(1) == 0
    pair = 2 * HEAD_DIM
    low_head = lax.broadcasted_iota(jnp.int32, (cq, pair), 1) < HEAD_DIM

    def chunk(cl, carry):
        r0 = pl.multiple_of(cl * cq, cq)
        q = q_ref[pl.ds(r0, cq), :] * ATTN_SCALE
        kw = kcat[pl.ds(r0, span), :]
        vw = vcat[pl.ds(r0, span), :]
        if mask_first:
            col = lax.broadcasted_iota(jnp.int32, (cq, span), 1)
            valid = jnp.logical_or(jnp.logical_not(first), col + r0 >= BAND)
        scores = []
        for pr in range(N_HEADS // 2):
            sl = slice(pr * pair, (pr + 1) * pair)
            q_pair = jnp.concatenate([jnp.where(low_head, q[:, sl], 0.0), jnp.where(low_head, 0.0, q[:, sl])],
                                     axis=0).astype(_bf16)
            s_pair = lax.dot_general(q_pair, kw[:, sl], (((1,), (1,)), ((), ())), preferred_element_type=_f32)
            scores += [s_pair[:cq], s_pair[cq:]]
        probs, inv = [], []
        for h in range(N_HEADS):
            s = scores[h] + bias_ref[h]
            if mask_first:
                s = jnp.where(valid, s, NEG_INF)
            p = jnp.exp(s - jnp.max(s, axis=-1, keepdims=True))
            inv.append(1.0 / jnp.sum(p, axis=-1, keepdims=True))
            probs.append(p.astype(_bf16))
        outs = []
        for pr in range(N_HEADS // 2):
            vp = vw[:, pr * pair:(pr + 1) * pair]
            o_pair = jnp.dot(jnp.concatenate([probs[2 * pr], probs[2 * pr + 1]], axis=0), vp,
                             preferred_element_type=_f32)
            outs.append(jnp.where(low_head, o_pair[:cq] * inv[2 * pr], o_pair[cq:] * inv[2 * pr + 1]))
        o_ref[pl.ds(r0, cq), :] = jnp.concatenate(outs, axis=-1).astype(o_ref.dtype)
        return carry

    lax.fori_loop(0, n_chunks, chunk, 0)


def _attention(q_src, kv_src, kprev_src, vprev_src, bias, *, out_rows, n_chunks, cq, mask_first, grid,
               q_map, k_map, v_map, kprev_spec, vprev_spec):
    nq = n_chunks * cq
    d = N_HEADS * HEAD_DIM
    span = BAND + cq
    kern = functools.partial(_attn_kernel, n_chunks=n_chunks, cq=cq, mask_first=mask_first)
    return pl.pallas_call(
        kern,
        grid=grid,
        in_specs=[pl.BlockSpec((nq, d), q_map),
                  kprev_spec,
                  pl.BlockSpec((nq, d), k_map),
                  vprev_spec,
                  pl.BlockSpec((nq, d), v_map),
                  pl.BlockSpec((N_HEADS, cq, span), lambda b, i: (0, 0, 0))],
        out_specs=pl.BlockSpec((nq, d), lambda b, i, g=grid: (b * g[1] + i, 0)),
        out_shape=jax.ShapeDtypeStruct((out_rows, d), _bf16),
        scratch_shapes=[pltpu.VMEM((BAND + nq, d), _bf16), pltpu.VMEM((BAND + nq, d), _bf16)],
        compiler_params=pltpu.CompilerParams(
            dimension_semantics=("parallel", "arbitrary"), vmem_limit_bytes=VMEM_LIMIT),
        name="band_attention",
    )(q_src, kprev_src, kv_src, vprev_src, kv_src, bias)


def _shift_rows(x, s, fill, row):
    return jnp.where(row >= s, pltpu.roll(x, s, axis=0), fill)


def _lru_kernel(x_ref, gb_ref, cinit_ref, h0_ref, cw_ref, cb_ref, wr_ref, br_ref, wi_ref, bi_ref,
                lam_ref, o_ref, hl_ref, xbuf, hcar, *, tt):
    d = x_ref.shape[-1]

    @pl.when(pl.program_id(1) == 0)
    def _():
        xbuf[0:SUBLANES, :] = cinit_ref[...]
        hcar[...] = h0_ref[...]

    xbuf[SUBLANES:SUBLANES + tt, :] = x_ref[...]
    base = SUBLANES - (CONV_W - 1)
    xc = cb_ref[...]
    for k in range(CONV_W):
        xc = xc + xbuf[pl.ds(base + k, tt), :] * cw_ref[k:k + 1, :]
    xbuf[0:SUBLANES, :] = xbuf[tt:tt + SUBLANES, :]

    xcb = xc.astype(_bf16)
    n_groups = d // MXU_DIM
    r_parts, i_parts = [], []
    for g in range(n_groups):
        xs = xcb[:, g * MXU_DIM:(g + 1) * MXU_DIM]
        r_parts.append(jnp.dot(xs, wr_ref[g], preferred_element_type=_f32))
        i_parts.append(jnp.dot(xs, wi_ref[g], preferred_element_type=_f32))
    r = jax.nn.sigmoid(jnp.concatenate(r_parts, axis=-1) + br_ref[...])
    gi = jax.nn.sigmoid(jnp.concatenate(i_parts, axis=-1) + bi_ref[...])
    z = -lam_ref[...]
    softplus = jnp.maximum(z, 0.0) + jnp.log(1.0 + jnp.exp(-jnp.abs(z)))
    log_a = (-LRU_C) * r * softplus
    a = jnp.exp(log_a)
    b = jnp.sqrt(1.0 - a * a) * (gi * xc)

    row = lax.broadcasted_iota(jnp.int32, (tt, d), 0) % SUBLANES
    s = 1
    while s < SUBLANES:
        b = a * _shift_rows(b, s, 0.0, row) + b
        a = a * _shift_rows(a, s, 1.0, row)
        s *= 2
    h_in = hcar[...]
    groups = []
    for g in range(tt // SUBLANES):
        rows = slice(g * SUBLANES, (g + 1) * SUBLANES)
        h_g = a[rows] * h_in + b[rows]
        groups.append(h_g)
        h_in = h_g[SUBLANES - 1:SUBLANES, :]
    h = jnp.concatenate(groups, axis=0)
    h_last = h[tt - 1:tt, :]
    hcar[...] = h_last
    hl_ref[...] = h_last
    o_ref[...] = (h * jax.nn.gelu(gb_ref[...].astype(_f32))).astype(o_ref.dtype)


def _lru(proj32, proj16, conv_init, h0, weights, *, n_batch, seq, tt, row0):
    d = conv_init.shape[-1]
    steps = seq // tt
    blk0 = row0 // tt
    (cw, cb, wr, br, wi, bi, lam) = weights
    const2 = lambda b, i: (0, 0)
    const3 = lambda b, i: (0, 0, 0)
    kern = functools.partial(_lru_kernel, tt=tt)
    return pl.pallas_call(
        kern,
        grid=(n_batch, steps),
        in_specs=[pl.BlockSpec((tt, d), lambda b, i: (blk0 + b * steps + i, XR_COL)),
                  pl.BlockSpec((tt, d), lambda b, i: (blk0 + b * steps + i, GB_COL)),
                  pl.BlockSpec((None, SUBLANES, d), lambda b, i: (b, 0, 0)),
                  pl.BlockSpec((None, 1, d), lambda b, i: (b, 0, 0)),
                  pl.BlockSpec((CONV_W, d), const2),
                  pl.BlockSpec((1, d), const2),
                  pl.BlockSpec(wr.shape, const3),
                  pl.BlockSpec((1, d), const2),
                  pl.BlockSpec(wi.shape, const3),
                  pl.BlockSpec((1, d), const2),
                  pl.BlockSpec((1, d), const2)],
        out_specs=[pl.BlockSpec((tt, d), lambda b, i: (b * steps + i, 0)),
                   pl.BlockSpec((None, 1, d), lambda b, i: (b, 0, 0))],
        out_shape=[jax.ShapeDtypeStruct((n_batch * seq, d), _bf16),
                   jax.ShapeDtypeStruct((n_batch, 1, d), _f32)],
        scratch_shapes=[pltpu.VMEM((tt + 2 * SUBLANES, d), _f32), pltpu.VMEM((1, d), _f32)],
        compiler_params=pltpu.CompilerParams(
            dimension_semantics=("parallel", "arbitrary"), vmem_limit_bytes=VMEM_LIMIT),
        name="conv_rglru",
    )(proj32, proj16, conv_init, h0, cw, cb, wr, br, wi, bi, lam)


def _merge_kernel(x_ref, oa_ref, ol_ref, ga_ref, gr_ref, wba_ref, wbl_ref, wo_ref, gn_ref, wq_ref,
                  x1_ref, xt_ref, qp_ref):
    ya = jnp.dot(oa_ref[...], wba_ref[...], preferred_element_type=_f32)
    yl = jnp.dot(ol_ref[...], wbl_ref[...], preferred_element_type=_f32)
    merged = (jax.nn.sigmoid(ga_ref[...].astype(_f32)) * ya
              + jax.nn.sigmoid(gr_ref[...].astype(_f32)) * yl)
    x1 = x_ref[...] + jnp.dot(merged.astype(_bf16), wo_ref[...], preferred_element_type=_f32)
    x1_ref[...] = x1
    xt = _rms(x1, gn_ref[...])
    xt_ref[...] = xt.reshape(xt_ref.shape)
    qp_ref[...] = jnp.dot(xt.astype(_bf16), wq_ref[...], preferred_element_type=_f32).astype(_bf16)


def _merge(x, o_attn, o_lru, proj16, w_ba, w_bl, w_o, g_ffn, wq):
    n, d = x.shape
    tm = _row_tile(n)
    nq = wq.shape[1]
    row = lambda i: (i, 0)
    const = lambda i: (0, 0)
    return pl.pallas_call(
        _merge_kernel,
        grid=(n // tm,),
        in_specs=[pl.BlockSpec((tm, d), row),
                  pl.BlockSpec((tm, d), row),
                  pl.BlockSpec((tm, d), row),
                  pl.BlockSpec((tm, d), lambda i: (i, GA_COL)),
                  pl.BlockSpec((tm, d), lambda i: (i, GR_COL)),
                  pl.BlockSpec((d, d), const, pipeline_mode=pl.Buffered(1)),
                  pl.BlockSpec((d, d), const, pipeline_mode=pl.Buffered(1)),
                  pl.BlockSpec((d, d), const, pipeline_mode=pl.Buffered(1)),
                  pl.BlockSpec((1, d), const),
                  pl.BlockSpec((d, nq), const, pipeline_mode=pl.Buffered(1))],
        out_specs=[pl.BlockSpec((tm, d), row), pl.BlockSpec((tm, d // LANES, LANES), lambda i: (i, 0, 0)),
                   pl.BlockSpec((tm, nq), row)],
        out_shape=[jax.ShapeDtypeStruct((n, d), _f32), jax.ShapeDtypeStruct((n, d // LANES, LANES), _f32),
                   jax.ShapeDtypeStruct((n, nq), _bf16)],
        compiler_params=pltpu.CompilerParams(
            dimension_semantics=("parallel",), vmem_limit_bytes=VMEM_LIMIT),
        name="merge_outproj_peerq",
    )(x, o_attn, o_lru, proj16, proj16, w_ba, w_bl, w_o, g_ffn, wq)


def _top16(s, payload=None):
    n_rows = s.shape[0]
    row = lax.broadcasted_iota(jnp.int32, s.shape, 0).astype(_f32)
    groups = [slice(g, g + SUBLANES) for g in range(0, n_rows, SUBLANES)]
    vals, picks = [], []
    for _ in range(TOPK):
        level = [(s[g], row[g], None if payload is None else payload[g]) for g in groups]
        while len(level) > 1:
            nxt = []
            for k in range(0, len(level) - 1, 2):
                (va, ia, pa), (vb, ib, pb) = level[k], level[k + 1]
                left = va >= vb
                nxt.append((jnp.maximum(va, vb), jnp.where(left, ia, ib),
                            None if pa is None else jnp.where(left, pa, pb)))
            if len(level) % 2:
                nxt.append(level[-1])
            level = nxt
        v8, i8, p8 = level[0]
        m = jnp.max(v8, axis=0, keepdims=True)
        am = jnp.min(jnp.where(v8 == m, i8, float(n_rows)), axis=0, keepdims=True)
        vals.append(m)
        if payload is None:
            picks.append(am)
        else:
            picks.append(jnp.max(jnp.where(i8 == am, p8, -1.0), axis=0, keepdims=True))
        s = jnp.where(row == am, -jnp.inf, s)
    return jnp.concatenate(vals, axis=0), jnp.concatenate(picks, axis=0)


def _candidates(a, b, mask_out):
    j8 = lax.broadcasted_iota(jnp.int32, (SUBLANES, a.shape[1]), 0)
    pieces = [a[0:1, :] + b]
    for i in range(1, SUBLANES):
        piece = a[i:i + 1, :] + b[0:SUBLANES, :]
        if mask_out:
            piece = jnp.where(j8 < TOPK // (i + 1), piece, -jnp.inf)
        pieces.append(piece)
    pieces.append(a[SUBLANES:TOPK, :] + b[0:1, :])
    return jnp.concatenate(pieces, axis=0)


def _topk_kernel(qp_ref, k1_ref, k2_ref, idx_ref, g_ref):
    nt = (((1,), (1,)), ((), ()))
    picks = []
    for h in range(PEER_HEADS):
        c0 = h * 2 * N_KEYS
        q1 = qp_ref[:, c0:c0 + N_KEYS]
        q2 = qp_ref[:, c0 + N_KEYS:c0 + 2 * N_KEYS]
        s1 = lax.dot_general(k1_ref[h], q1, nt, preferred_element_type=_f32)
        s2 = lax.dot_general(k2_ref[h], q2, nt, preferred_element_type=_f32)
        v1, i1 = _top16(s1)
        v2, i2 = _top16(s2)
        cand_s = _candidates(v1, v2, True)
        cand_i = _candidates(i1 * float(N_KEYS), i2, False)
        top_s, e_idx = _top16(cand_s, cand_i)
        p = jnp.exp(top_s - jnp.max(top_s, axis=0, keepdims=True))
        g = p / jnp.sum(p, axis=0, keepdims=True)
        picks.append(e_idx)
        g_ref[h * TOPK:(h + 1) * TOPK, :] = g
    idx_ref[...] = jnp.concatenate(picks, axis=0).T.astype(jnp.int32) * SLAB


def _topk(qp, keys1, keys2):
    n = qp.shape[0]
    nsel = PEER_HEADS * TOPK
    const3 = lambda i: (0, 0, 0)
    return pl.pallas_call(
        _topk_kernel,
        grid=(n // TOKEN_TILE,),
        in_specs=[pl.BlockSpec((TOKEN_TILE, qp.shape[1]), lambda i: (i, 0)),
                  pl.BlockSpec(keys1.shape, const3),
                  pl.BlockSpec(keys2.shape, const3)],
        out_specs=[pl.BlockSpec((TOKEN_TILE, nsel), lambda i: (i, 0)),
                   pl.BlockSpec((nsel, TOKEN_TILE), lambda i: (0, i))],
        out_shape=[jax.ShapeDtypeStruct((n, nsel), jnp.int32), jax.ShapeDtypeStruct((nsel, n), _f32)],
        compiler_params=pltpu.CompilerParams(
            dimension_semantics=("parallel",), vmem_limit_bytes=VMEM_LIMIT),
        name="peer_topk",
    )(qp, keys1, keys2)


def _pack_table(t):
    e, d = t.shape
    bits = lax.bitcast_convert_type(t.astype(_bf16), jnp.uint16).astype(jnp.uint32)
    words = bits[:, :d // 2] | (bits[:, d // 2:] << 16)
    return words.reshape(e * SLAB, LANES)


def _unpack(words):
    lo = pltpu.bitcast(words << 16, _f32)
    hi = pltpu.bitcast(words & jnp.uint32(0xFFFF0000), _f32)
    return lo, hi


def _gather_transposed(idx_ref, t, tab_ref, tile, slice_row):
    row = idx_ref.at[t] if slice_row else None
    for j in range(idx_ref.shape[1]):
        r = pl.multiple_of(row[j] if slice_row else idx_ref[t, j], SLAB)
        tile[pl.ds(j, SLAB, stride=SLAB_STRIDE), :] = tab_ref[pl.ds(r, SLAB), :]


def _tile_chunk(tile, s, nsel):
    return _unpack(tile[s * SLAB_STRIDE:s * SLAB_STRIDE + nsel, :])


def _split_bf16(x):
    hi = x.astype(_bf16)
    return hi, (x - hi.astype(_f32)).astype(_bf16)


def _mxu_right(x, rhs):
    hi, lo = _split_bf16(x)
    return (jnp.dot(hi, rhs, preferred_element_type=_f32) + jnp.dot(lo, rhs, preferred_element_type=_f32))


REDUCE_GROUP = 32
U_TOKENS_PER_STEP = 2


def _peer_u_kernel(idx_ref, xt_ref, g_ref, tab_ref, w_ref, *scratch):
    tiles, parts = scratch[:-1], scratch[-1]
    nsel = idx_ref.shape[1]

    def partial_dots(t, tile):
        x = xt_ref[t]
        acc = jnp.zeros((nsel, LANES), _f32)
        for s in range(SLAB):
            lo, hi = _tile_chunk(tile, s, nsel)
            acc = acc + lo * x[s:s + 1, :] + hi * x[SLAB + s:SLAB + s + 1, :]
        return acc

    n_t = len(tiles)
    last = tiles[-1]
    last[...] = jnp.zeros_like(last)

    def step(i, carry):
        t = n_t * i
        t_prev = jnp.maximum(t - 1, 0)
        parts[t_prev] = partial_dots(t_prev, last)
        for k, tile in enumerate(tiles):
            _gather_transposed(idx_ref, t + k, tab_ref, tile, slice_row=False)
        for k, tile in enumerate(tiles[:-1]):
            parts[t + k] = partial_dots(t + k, tile)
        return carry

    lax.fori_loop(0, TOKEN_TILE // n_t, step, 0)
    parts[TOKEN_TILE - 1] = partial_dots(TOKEN_TILE - 1, last)

    lane = lax.broadcasted_iota(jnp.int32, (nsel, TOKEN_TILE), 1)
    ones = jnp.ones((LANES, LANES), _bf16)

    def reduce_group(gi, act):
        for k in range(REDUCE_GROUP):
            t = gi * REDUCE_GROUP + k
            act = jnp.where(lane == t, _mxu_right(parts[t], ones), act)
        return act

    act = lax.fori_loop(0, TOKEN_TILE // REDUCE_GROUP, reduce_group, jnp.zeros((nsel, TOKEN_TILE), _f32))
    w_ref[...] = g_ref[...] * jax.nn.gelu(act)


def _peer_u(idx, xt, g_t, table):
    n = xt.shape[0]
    d = xt.shape[1] * xt.shape[2]
    nsel = idx.shape[1]
    return pl.pallas_call(
        _peer_u_kernel,
        grid=(n // TOKEN_TILE,),
        in_specs=[pl.BlockSpec((TOKEN_TILE, nsel), lambda i: (i, 0), memory_space=pltpu.SMEM),
                  pl.BlockSpec((TOKEN_TILE, d // LANES, LANES), lambda i: (i, 0, 0)),
                  pl.BlockSpec((nsel, TOKEN_TILE), lambda i: (0, i)),
                  _resident_table_spec(table)],
        out_specs=pl.BlockSpec((nsel, TOKEN_TILE), lambda i: (0, i)),
        out_shape=jax.ShapeDtypeStruct((nsel, n), _f32),
        scratch_shapes=[pltpu.VMEM((SLAB * SLAB_STRIDE, LANES), jnp.uint32)] * U_TOKENS_PER_STEP
                       + [pltpu.VMEM((TOKEN_TILE, nsel, LANES), _f32)],
        compiler_params=pltpu.CompilerParams(
            dimension_semantics=("arbitrary",), vmem_limit_bytes=VMEM_LIMIT),
        name="peer_u_pass",
    )(idx, xt, g_t, table)


def _resident_table_spec(table):
    return pl.BlockSpec(table.shape, lambda i: (0, 0), pipeline_mode=pl.Buffered(1))


def _peer_v_kernel(idx_ref, w_ref, tab_ref, o_ref, tile_a, tile_b, wcol_b):
    nsel = idx_ref.shape[1]
    lane = lax.broadcasted_iota(jnp.int32, (nsel, TOKEN_TILE), 1)

    def weight_column(t):
        w_col = jnp.sum(jnp.where(lane == t, w_ref[...], 0.0), axis=1, keepdims=True)
        return jnp.broadcast_to(w_col, (nsel, LANES))

    def weighted_sum(t, tile, w_col):
        rows_lo, rows_hi = [], []
        for s in range(SLAB):
            lo, hi = _tile_chunk(tile, s, nsel)
            rows_lo.append(jnp.sum(lo * w_col, axis=0, keepdims=True))
            rows_hi.append(jnp.sum(hi * w_col, axis=0, keepdims=True))
        o_ref[t] = jnp.concatenate(rows_lo + rows_hi, axis=0)

    tile_b[...] = jnp.zeros_like(tile_b)
    wcol_b[...] = jnp.zeros_like(wcol_b)

    def two_tokens(i, carry):
        t = 2 * i
        w_a = weight_column(t)
        w_b = weight_column(t + 1)
        weighted_sum(jnp.maximum(t - 1, 0), tile_b, wcol_b[...])
        _gather_transposed(idx_ref, t, tab_ref, tile_a, slice_row=True)
        _gather_transposed(idx_ref, t + 1, tab_ref, tile_b, slice_row=True)
        weighted_sum(t, tile_a, w_a)
        wcol_b[...] = w_b
        return carry

    lax.fori_loop(0, TOKEN_TILE // 2, two_tokens, 0)
    weighted_sum(TOKEN_TILE - 1, tile_b, wcol_b[...])


def _peer_v(idx, w_t, table, d):
    n, nsel = idx.shape
    return pl.pallas_call(
        _peer_v_kernel,
        grid=(n // TOKEN_TILE,),
        in_specs=[pl.BlockSpec((TOKEN_TILE, nsel), lambda i: (i, 0), memory_space=pltpu.SMEM),
                  pl.BlockSpec((nsel, TOKEN_TILE), lambda i: (0, i)),
                  _resident_table_spec(table)],
        out_specs=pl.BlockSpec((TOKEN_TILE, d // LANES, LANES), lambda i: (i, 0, 0)),
        out_shape=jax.ShapeDtypeStruct((n, d // LANES, LANES), _f32),
        scratch_shapes=[pltpu.VMEM((SLAB * SLAB_STRIDE, LANES), jnp.uint32),
                        pltpu.VMEM((SLAB * SLAB_STRIDE, LANES), jnp.uint32),
                        pltpu.VMEM((nsel, LANES), _f32)],
        compiler_params=pltpu.CompilerParams(
            dimension_semantics=("arbitrary",), vmem_limit_bytes=VMEM_LIMIT),
        name="peer_v_pass",
    )(idx, w_t, table)


def _final_kernel(x1_ref, p_ref, g_ref, y_ref):
    y_ref[...] = _rms(x1_ref[...] + p_ref[...].reshape(x1_ref.shape), g_ref[...])


def _final(x1, peer, g):
    n, d = x1.shape
    tm = _row_tile(n)
    row = lambda i: (i, 0)
    return pl.pallas_call(
        _final_kernel,
        grid=(n // tm,),
        in_specs=[pl.BlockSpec((tm, d), row), pl.BlockSpec((tm, d // LANES, LANES), lambda i: (i, 0, 0)),
                  pl.BlockSpec((1, d), lambda i: (0, 0))],
        out_specs=pl.BlockSpec((tm, d), row),
        out_shape=jax.ShapeDtypeStruct((n, d), _f32),
        compiler_params=pltpu.CompilerParams(
            dimension_semantics=("parallel",), vmem_limit_bytes=VMEM_LIMIT),
        name="residual_final_norm",
    )(x1, peer, g)


def _rel_bias(rel_table, n_q, n_k, offset):
    m = jnp.arange(n_k + n_q - 1)
    diag = rel_table[:, jnp.clip(offset + n_q - 1 - m, -MAX_REL, MAX_REL) + MAX_REL].astype(_f32)
    return jnp.stack([diag[:, n_q - 1 - q:n_q - 1 - q + n_k] for q in range(n_q)], axis=1)


def _block_diag_groups(w):
    nb, bi, bo = w.shape
    per = MXU_DIM // bi
    w = w.reshape(nb // per, per, bi, bo)
    eye = jnp.eye(per, dtype=w.dtype)
    return jnp.einsum("gpio,pq->gpiqo", w, eye).reshape(nb // per, per * bi, per * bo)


def _layer(xp, xs, cache_k, cache_v, state_conv, state_lru, norm_mix, w_in, rel_table, conv_w, conv_b,
           lru_wr, lru_br, lru_wi, lru_bi, lru_lambda, w_ba, w_bl, w_o, norm_ffn, peer_wq, keys1, keys2,
           tab_u, tab_v, norm_final):
    bsz, seq, d = xp.shape
    dbsz, dseq, _ = xs.shape
    n_p, n_s = bsz * seq, dbsz * dseq
    xp2, xs2 = xp.reshape(n_p, d), xs.reshape(n_s, d)
    row = lambda v: v.reshape(1, -1)

    w_in_b = w_in.astype(_bf16)
    proj, proj16 = _inproj(xp2, row(norm_mix), w_in_b)
    proj_s, proj16_s = _inproj(xs2, row(norm_mix), w_in_b)

    tiles = seq // BAND
    bias_p = _rel_bias(rel_table, CHUNK, BAND + CHUNK, BAND)
    prev = lambda c: (lambda b, i: (b * tiles + jnp.maximum(i - 1, 0), c))
    cur = lambda c: (lambda b, i: (b * tiles + i, c))
    o_attn_p = _attention(
        proj16, proj, proj, proj, bias_p, out_rows=n_p, n_chunks=BAND // CHUNK, cq=CHUNK, mask_first=True,
        grid=(bsz, tiles), q_map=cur(Q_COL), k_map=cur(K_COL), v_map=cur(V_COL),
        kprev_spec=pl.BlockSpec((BAND, d), prev(K_COL)), vprev_spec=pl.BlockSpec((BAND, d), prev(V_COL)))
    rows_c = cache_k.shape[1]
    bias_s = _rel_bias(rel_table, dseq, rows_c + dseq, rows_c)
    samp = lambda c: (lambda b, i: (b, c))
    cache_spec = pl.BlockSpec((None, rows_c, d), lambda b, i: (b, 0, 0))
    o_attn_s = _attention(
        proj16_s, proj_s, cache_k.reshape(dbsz, rows_c, d), cache_v.reshape(dbsz, rows_c, d), bias_s,
        out_rows=n_s, n_chunks=1, cq=dseq, mask_first=False, grid=(dbsz, 1),
        q_map=samp(Q_COL), k_map=samp(K_COL), v_map=samp(V_COL), kprev_spec=cache_spec, vprev_spec=cache_spec)

    lru_w = (conv_w, row(conv_b), _block_diag_groups(lru_wr).astype(_bf16), row(lru_br),
             _block_diag_groups(lru_wi).astype(_bf16), row(lru_bi), row(lru_lambda))
    pad_state = lambda st: jnp.pad(st, ((0, 0), (SUBLANES - (CONV_W - 1), 0), (0, 0)))
    o_lru_p, h_p = _lru(proj, proj16, jnp.zeros((bsz, SUBLANES, d), _f32), jnp.zeros((bsz, 1, d), _f32), lru_w,
                        n_batch=bsz, seq=seq, tt=256, row0=0)
    o_lru_s, h_s = _lru(proj_s, proj16_s, pad_state(state_conv), state_lru.reshape(dbsz, 1, d), lru_w,
                        n_batch=dbsz, seq=dseq, tt=dseq, row0=0)

    mix_w = (w_ba.astype(_bf16), w_bl.astype(_bf16), w_o.astype(_bf16), row(norm_ffn), peer_wq.astype(_bf16))
    k1, k2 = keys1.astype(_bf16), keys2.astype(_bf16)

    def ffn(x2, o_attn, o_lru, pr):
        x1, xt, qp = _merge(x2, o_attn, o_lru, pr, *mix_w)
        idx, g_t = _topk(qp, k1, k2)
        w_t = _peer_u(idx, xt, g_t, tab_u)
        peer = _peer_v(idx, w_t, tab_v, d)
        return _final(x1, peer, row(norm_final))

    y_p = ffn(xp2, o_attn_p, o_lru_p, proj16).reshape(bsz, seq, d)
    y_s = ffn(xs2, o_attn_s, o_lru_s, proj16_s).reshape(dbsz, dseq, d)

    def window(pr, b, t, rows, c):
        return pr.reshape(b, t, -1)[:, t - rows:, c * d:(c + 1) * d]

    heads = lambda a: a.reshape(a.shape[0], a.shape[1], N_HEADS, HEAD_DIM)
    rows_p = min(BAND, seq)
    keep = CONV_W - 1
    xr_s = jnp.concatenate([state_conv, window(proj_s, dbsz, dseq, dseq, XR_COL)], axis=1)
    return (y_p, y_s, heads(window(proj, bsz, seq, rows_p, K_COL)), heads(window(proj, bsz, seq, rows_p, V_COL)),
            window(proj, bsz, seq, keep, XR_COL), h_p.reshape(bsz, d),
            heads(window(proj_s, dbsz, dseq, dseq, K_COL)), heads(window(proj_s, dbsz, dseq, dseq, V_COL)),
            xr_s[:, xr_s.shape[1] - keep:], h_s.reshape(dbsz, d))


def kernel(x_prompt, x_sample, cache_k, cache_v, state_conv, state_lru, norm_mix, w_in, rel_table, conv_w, conv_b, lru_wr, lru_br, lru_wi, lru_bi, lru_lambda, w_branch_attn, w_branch_lru, w_out, norm_ffn, peer_wq, peer_keys1, peer_keys2, peer_u, peer_v, norm_final):
    depth = w_in.shape[0]
    assert depth == 1, "the final norm is fused into the single layer's last kernel"
    l = 0
    (y_prompt, y_sample, k_p, v_p, c_p, h_p, k_s, v_s, c_s, h_s) = _layer(
        x_prompt, x_sample, cache_k[l], cache_v[l], state_conv[l], state_lru[l], norm_mix[l], w_in[l],
        rel_table[l], conv_w[l], conv_b[l], lru_wr[l], lru_br[l], lru_wi[l], lru_bi[l], lru_lambda[l],
        w_branch_attn[l], w_branch_lru[l], w_out[l], norm_ffn[l], peer_wq[l], peer_keys1[l], peer_keys2[l],
        _pack_table(peer_u[l]), _pack_table(peer_v[l]), norm_final)
    stack = lambda v: v[None]
    return (y_prompt, y_sample, stack(k_p), stack(v_p), stack(c_p), stack(h_p),
            stack(k_s), stack(v_s), stack(c_s), stack(h_s))
```

```python
import functools

import jax
import jax.numpy as jnp
from jax import lax
from jax.experimental import pallas as pl
from jax.experimental.pallas import tpu as pltpu

CHUNK = 64
LEFT_CHUNKS = 8
BAND = LEFT_CHUNKS * CHUNK
N_HEADS = 16
HEAD_DIM = 64
ATTN_SCALE = HEAD_DIM ** -0.5
MAX_REL = 128
CONV_W = 4
LRU_C = 8.0
PEER_HEADS = 8
N_KEYS = 128
TOPK = 16
EPS = 1e-6
NEG_INF = -1e30

LANES = 128
SUBLANES = 8
MXU_DIM = 256
TOKEN_TILE = 128
SLAB = 4
SLAB_STRIDE = 136
VMEM_LIMIT = 56 * 1024 * 1024

_bf16 = jnp.bfloat16
_f32 = jnp.float32


def _row_tile(n_rows):
    for t in (512, 384, 256, 128):
        if n_rows % t == 0:
            return t
    raise ValueError(f"token count {n_rows} must be a multiple of 128")


def _rms(x, g):
    return x * lax.rsqrt(jnp.mean(x * x, axis=-1, keepdims=True) + EPS) * g


F32_BLOCKS = (1, 2, 3)
BF16_BLOCKS = (0, 4, 5, 6)
K_COL, V_COL, XR_COL = 0, 1, 2
Q_COL, GB_COL, GA_COL, GR_COL = 0, 1, 2, 3


def _inproj_kernel(x_ref, g_ref, w_ref, o32_ref, o16_ref):
    d = x_ref.shape[1]
    xn = _rms(x_ref[...], g_ref[...]).astype(_bf16)
    for j in range(w_ref.shape[1] // d):
        y = jnp.dot(xn, w_ref[:, j * d:(j + 1) * d], preferred_element_type=_f32)
        if j in F32_BLOCKS:
            c = F32_BLOCKS.index(j)
            o32_ref[:, c * d:(c + 1) * d] = y
        else:
            c = BF16_BLOCKS.index(j)
            o16_ref[:, c * d:(c + 1) * d] = y.astype(_bf16)


def _inproj(x, g, w):
    n, d = x.shape
    cols = w.shape[1]
    assert cols == (len(F32_BLOCKS) + len(BF16_BLOCKS)) * d
    tm = 256 if n % 256 == 0 else TOKEN_TILE
    n32, n16 = len(F32_BLOCKS) * d, len(BF16_BLOCKS) * d
    return pl.pallas_call(
        _inproj_kernel,
        grid=(n // tm,),
        in_specs=[pl.BlockSpec((tm, d), lambda i: (i, 0)),
                  pl.BlockSpec((1, d), lambda i: (0, 0)),
                  pl.BlockSpec((d, cols), lambda i: (0, 0), pipeline_mode=pl.Buffered(1))],
        out_specs=[pl.BlockSpec((tm, n32), lambda i: (i, 0)), pl.BlockSpec((tm, n16), lambda i: (i, 0))],
        out_shape=[jax.ShapeDtypeStruct((n, n32), _f32), jax.ShapeDtypeStruct((n, n16), _bf16)],
        compiler_params=pltpu.CompilerParams(
            dimension_semantics=("parallel",), vmem_limit_bytes=VMEM_LIMIT),
        name="inproj",
    )(x, g, w)


def _attn_kernel(q_ref, kp_ref, kc_ref, vp_ref, vc_ref, bias_ref, o_ref, kcat, vcat,
                 *, n_chunks, cq, mask_first):
    nq = n_chunks * cq
    span = BAND + cq
    kcat[0:BAND, :] = kp_ref[...].astype(_bf16)
    kcat[BAND:BAND + nq, :] = kc_ref[...].astype(_bf16)
    vcat[0:BAND, :] = vp_ref[...].astype(_bf16)
    vcat[BAND:BAND + nq, :] = vc_ref[...].astype(_bf16)
    first = pl.program_id(1) == 0
    pair = 2 * HEAD_DIM
    low_head = lax.broadcasted_iota(jnp.int32, (cq, pair), 1) < HEAD_DIM

    def chunk(cl, carry, masked):
        r0 = pl.multiple_of(cl * cq, cq)
        q = q_ref[pl.ds(r0, cq), :] * ATTN_SCALE
        kw = kcat[pl.ds(r0, span), :]
        vw = vcat[pl.ds(r0, span), :]
        if masked:
            valid = lax.broadcasted_iota(jnp.int32, (cq, span), 1) + r0 >= BAND
        scores = []
        for pr in range(N_HEADS // 2):
            sl = slice(pr * pair, (pr + 1) * pair)
            q_pair = jnp.concatenate([jnp.where(low_head, q[:, sl], 0.0), jnp.where(low_head, 0.0, q[:, sl])],
                                     axis=0).astype(_bf16)
            s_pair = lax.dot_general(q_pair, kw[:, sl], (((1,), (1,)), ((), ())), preferred_element_type=_f32)
            scores += [s_pair[:cq], s_pair[cq:]]
        probs, inv = [], []
        for h in range(N_HEADS):
            s = scores[h] + bias_ref[h]
            if masked:
                s = jnp.where(valid, s, NEG_INF)
            p = jnp.exp(s - jnp.max(s, axis=-1, keepdims=True))
            inv.append(1.0 / jnp.sum(p, axis=-1, keepdims=True))
            probs.append(p.astype(_bf16))
        outs = []
        for pr in range(N_HEADS // 2):
            vp = vw[:, pr * pair:(pr + 1) * pair]
            o_pair = jnp.dot(jnp.concatenate([probs[2 * pr], probs[2 * pr + 1]], axis=0), vp,
                             preferred_element_type=_f32)
            outs.append(jnp.where(low_head, o_pair[:cq] * inv[2 * pr], o_pair[cq:] * inv[2 * pr + 1]))
        o_ref[pl.ds(r0, cq), :] = jnp.concatenate(outs, axis=-1).astype(o_ref.dtype)
        return carry

    def run(masked):
        lax.fori_loop(0, n_chunks, functools.partial(chunk, masked=masked), 0)

    if mask_first:
        pl.when(first)(functools.partial(run, True))
        pl.when(jnp.logical_not(first))(functools.partial(run, False))
    else:
        run(False)


def _attention(q_src, kv_src, kprev_src, vprev_src, bias, *, out_rows, n_chunks, cq, mask_first, grid,
               q_map, k_map, v_map, kprev_spec, vprev_spec):
    nq = n_chunks * cq
    d = N_HEADS * HEAD_DIM
    span = BAND + cq
    kern = functools.partial(_attn_kernel, n_chunks=n_chunks, cq=cq, mask_first=mask_first)
    return pl.pallas_call(
        kern,
        grid=grid,
        in_specs=[pl.BlockSpec((nq, d), q_map),
                  kprev_spec,
                  pl.BlockSpec((nq, d), k_map),
                  vprev_spec,
                  pl.BlockSpec((nq, d), v_map),
                  pl.BlockSpec((N_HEADS, cq, span), lambda b, i: (0, 0, 0))],
        out_specs=pl.BlockSpec((nq, d), lambda b, i, g=grid: (b * g[1] + i, 0)),
        out_shape=jax.ShapeDtypeStruct((out_rows, d), _bf16),
        scratch_shapes=[pltpu.VMEM((BAND + nq, d), _bf16), pltpu.VMEM((BAND + nq, d), _bf16)],
        compiler_params=pltpu.CompilerParams(
            dimension_semantics=("parallel", "arbitrary"), vmem_limit_bytes=VMEM_LIMIT),
        name="band_attention",
    )(q_src, kprev_src, kv_src, vprev_src, kv_src, bias)


def _shift_rows(x, s, fill, row):
    return jnp.where(row >= s, pltpu.roll(x, s, axis=0), fill)


def _lru_kernel(x_ref, gb_ref, cinit_ref, h0_ref, cw_ref, cb_ref, wr_ref, br_ref, wi_ref, bi_ref,
                lam_ref, o_ref, hl_ref, xbuf, hcar, *, tt):
    d = x_ref.shape[-1]

    @pl.when(pl.program_id(1) == 0)
    def _():
        xbuf[0:SUBLANES, :] = cinit_ref[...]
        hcar[...] = h0_ref[...]

    xbuf[SUBLANES:SUBLANES + tt, :] = x_ref[...]
    base = SUBLANES - (CONV_W - 1)
    xc = cb_ref[...]
    for k in range(CONV_W):
        xc = xc + xbuf[pl.ds(base + k, tt), :] * cw_ref[k:k + 1, :]
    xbuf[0:SUBLANES, :] = xbuf[tt:tt + SUBLANES, :]

    xcb = xc.astype(_bf16)
    n_groups = d // MXU_DIM
    r_parts, i_parts = [], []
    for g in range(n_groups):
        xs = xcb[:, g * MXU_DIM:(g + 1) * MXU_DIM]
        r_parts.append(jnp.dot(xs, wr_ref[g], preferred_element_type=_f32))
        i_parts.append(jnp.dot(xs, wi_ref[g], preferred_element_type=_f32))
    r = jax.nn.sigmoid(jnp.concatenate(r_parts, axis=-1) + br_ref[...])
    gi = jax.nn.sigmoid(jnp.concatenate(i_parts, axis=-1) + bi_ref[...])
    z = -lam_ref[...]
    softplus = jnp.maximum(z, 0.0) + jnp.log(1.0 + jnp.exp(-jnp.abs(z)))
    log_a = (-LRU_C) * r * softplus
    a = jnp.exp(log_a)
    b = jnp.sqrt(1.0 - a * a) * (gi * xc)

    row = lax.broadcasted_iota(jnp.int32, (tt, d), 0) % SUBLANES
    s = 1
    while s < SUBLANES:
        b = a * _shift_rows(b, s, 0.0, row) + b
        a = a * _shift_rows(a, s, 1.0, row)
        s *= 2
    h_in = hcar[...]
    groups = []
    for g in range(tt // SUBLANES):
        rows = slice(g * SUBLANES, (g + 1) * SUBLANES)
        h_g = a[rows] * h_in + b[rows]
        groups.append(h_g)
        h_in = h_g[SUBLANES - 1:SUBLANES, :]
    h = jnp.concatenate(groups, axis=0)
    h_last = h[tt - 1:tt, :]
    hcar[...] = h_last
    hl_ref[...] = h_last
    o_ref[...] = (h * jax.nn.gelu(gb_ref[...].astype(_f32))).astype(o_ref.dtype)


def _lru(proj32, proj16, conv_init, h0, weights, *, n_batch, seq, tt, row0):
    d = conv_init.shape[-1]
    steps = seq // tt
    blk0 = row0 // tt
    (cw, cb, wr, br, wi, bi, lam) = weights
    const2 = lambda b, i: (0, 0)
    const3 = lambda b, i: (0, 0, 0)
    kern = functools.partial(_lru_kernel, tt=tt)
    return pl.pallas_call(
        kern,
        grid=(n_batch, steps),
        in_specs=[pl.BlockSpec((tt, d), lambda b, i: (blk0 + b * steps + i, XR_COL)),
                  pl.BlockSpec((tt, d), lambda b, i: (blk0 + b * steps + i, GB_COL)),
                  pl.BlockSpec((None, SUBLANES, d), lambda b, i: (b, 0, 0)),
                  pl.BlockSpec((None, 1, d), lambda b, i: (b, 0, 0)),
                  pl.BlockSpec((CONV_W, d), const2),
                  pl.BlockSpec((1, d), const2),
                  pl.BlockSpec(wr.shape, const3),
                  pl.BlockSpec((1, d), const2),
                  pl.BlockSpec(wi.shape, const3),
                  pl.BlockSpec((1, d), const2),
                  pl.BlockSpec((1, d), const2)],
        out_specs=[pl.BlockSpec((tt, d), lambda b, i: (b * steps + i, 0)),
                   pl.BlockSpec((None, 1, d), lambda b, i: (b, 0, 0))],
        out_shape=[jax.ShapeDtypeStruct((n_batch * seq, d), _bf16),
                   jax.ShapeDtypeStruct((n_batch, 1, d), _f32)],
        scratch_shapes=[pltpu.VMEM((tt + 2 * SUBLANES, d), _f32), pltpu.VMEM((1, d), _f32)],
        compiler_params=pltpu.CompilerParams(
            dimension_semantics=("parallel", "arbitrary"), vmem_limit_bytes=VMEM_LIMIT),
        name="conv_rglru",
    )(proj32, proj16, conv_init, h0, cw, cb, wr, br, wi, bi, lam)


def _merge_kernel(x_ref, oa_ref, ol_ref, ga_ref, gr_ref, wba_ref, wbl_ref, wo_ref, gn_ref, wq_ref,
                  x1_ref, xt_ref, qp_ref):
    ya = jnp.dot(oa_ref[...], wba_ref[...], preferred_element_type=_f32)
    yl = jnp.dot(ol_ref[...], wbl_ref[...], preferred_element_type=_f32)
    merged = (jax.nn.sigmoid(ga_ref[...].astype(_f32)) * ya
              + jax.nn.sigmoid(gr_ref[...].astype(_f32)) * yl)
    x1 = x_ref[...] + jnp.dot(merged.astype(_bf16), wo_ref[...], preferred_element_type=_f32)
    x1_ref[...] = x1
    xt = _rms(x1, gn_ref[...])
    xt_ref[...] = xt.reshape(xt_ref.shape)
    qp_ref[...] = jnp.dot(xt.astype(_bf16), wq_ref[...], preferred_element_type=_f32).astype(_bf16)


def _merge(x, o_attn, o_lru, proj16, w_ba, w_bl, w_o, g_ffn, wq):
    n, d = x.shape
    tm = _row_tile(n)
    nq = wq.shape[1]
    row = lambda i: (i, 0)
    const = lambda i: (0, 0)
    return pl.pallas_call(
        _merge_kernel,
        grid=(n // tm,),
        in_specs=[pl.BlockSpec((tm, d), row),
                  pl.BlockSpec((tm, d), row),
                  pl.BlockSpec((tm, d), row),
                  pl.BlockSpec((tm, d), lambda i: (i, GA_COL)),
                  pl.BlockSpec((tm, d), lambda i: (i, GR_COL)),
                  pl.BlockSpec((d, d), const, pipeline_mode=pl.Buffered(1)),
                  pl.BlockSpec((d, d), const, pipeline_mode=pl.Buffered(1)),
                  pl.BlockSpec((d, d), const, pipeline_mode=pl.Buffered(1)),
                  pl.BlockSpec((1, d), const),
                  pl.BlockSpec((d, nq), const, pipeline_mode=pl.Buffered(1))],
        out_specs=[pl.BlockSpec((tm, d), row), pl.BlockSpec((tm, d // LANES, LANES), lambda i: (i, 0, 0)),
                   pl.BlockSpec((tm, nq), row)],
        out_shape=[jax.ShapeDtypeStruct((n, d), _f32), jax.ShapeDtypeStruct((n, d // LANES, LANES), _f32),
                   jax.ShapeDtypeStruct((n, nq), _bf16)],
        compiler_params=pltpu.CompilerParams(
            dimension_semantics=("parallel",), vmem_limit_bytes=VMEM_LIMIT),
        name="merge_outproj_peerq",
    )(x, o_attn, o_lru, proj16, proj16, w_ba, w_bl, w_o, g_ffn, wq)


def _top16(s, payload=None):
    n_rows = s.shape[0]
    row = lax.broadcasted_iota(jnp.int32, s.shape, 0).astype(_f32)
    groups = [slice(g, g + SUBLANES) for g in range(0, n_rows, SUBLANES)]
    vals, picks = [], []
    for _ in range(TOPK):
        level = [(s[g], row[g], None if payload is None else payload[g]) for g in groups]
        while len(level) > 1:
            nxt = []
            for k in range(0, len(level) - 1, 2):
                (va, ia, pa), (vb, ib, pb) = level[k], level[k + 1]
                left = va >= vb
                nxt.append((jnp.maximum(va, vb), jnp.where(left, ia, ib),
                            None if pa is None else jnp.where(left, pa, pb)))
            if len(level) % 2:
                nxt.append(level[-1])
            level = nxt
        v8, i8, p8 = level[0]
        m = jnp.max(v8, axis=0, keepdims=True)
        am = jnp.min(jnp.where(v8 == m, i8, float(n_rows)), axis=0, keepdims=True)
        vals.append(m)
        if payload is None:
            picks.append(am)
        else:
            picks.append(jnp.max(jnp.where(i8 == am, p8, -1.0), axis=0, keepdims=True))
        s = jnp.where(row == am, -jnp.inf, s)
    return jnp.concatenate(vals, axis=0), jnp.concatenate(picks, axis=0)


def _candidates(a, b, mask_out):
    j8 = lax.broadcasted_iota(jnp.int32, (SUBLANES, a.shape[1]), 0)
    pieces = [a[0:1, :] + b]
    for i in range(1, SUBLANES):
        piece = a[i:i + 1, :] + b[0:SUBLANES, :]
        if mask_out:
            piece = jnp.where(j8 < TOPK // (i + 1), piece, -jnp.inf)
        pieces.append(piece)
    pieces.append(a[SUBLANES:TOPK, :] + b[0:1, :])
    return jnp.concatenate(pieces, axis=0)


def _topk_kernel(qp_ref, k1_ref, k2_ref, idx_ref, g_ref):
    nt = (((1,), (1,)), ((), ()))
    picks = []
    for h in range(PEER_HEADS):
        c0 = h * 2 * N_KEYS
        q1 = qp_ref[:, c0:c0 + N_KEYS]
        q2 = qp_ref[:, c0 + N_KEYS:c0 + 2 * N_KEYS]
        s1 = lax.dot_general(k1_ref[h], q1, nt, preferred_element_type=_f32)
        s2 = lax.dot_general(k2_ref[h], q2, nt, preferred_element_type=_f32)
        v1, i1 = _top16(s1)
        v2, i2 = _top16(s2)
        cand_s = _candidates(v1, v2, True)
        cand_i = _candidates(i1 * float(N_KEYS), i2, False)
        top_s, e_idx = _top16(cand_s, cand_i)
        p = jnp.exp(top_s - jnp.max(top_s, axis=0, keepdims=True))
        g = p / jnp.sum(p, axis=0, keepdims=True)
        picks.append(e_idx)
        g_ref[h * TOPK:(h + 1) * TOPK, :] = g
    idx_ref[...] = jnp.concatenate(picks, axis=0).T.astype(jnp.int32) * SLAB


def _topk(qp, keys1, keys2):
    n = qp.shape[0]
    nsel = PEER_HEADS * TOPK
    const3 = lambda i: (0, 0, 0)
    return pl.pallas_call(
        _topk_kernel,
        grid=(n // TOKEN_TILE,),
        in_specs=[pl.BlockSpec((TOKEN_TILE, qp.shape[1]), lambda i: (i, 0)),
                  pl.BlockSpec(keys1.shape, const3),
                  pl.BlockSpec(keys2.shape, const3)],
        out_specs=[pl.BlockSpec((TOKEN_TILE, nsel), lambda i: (i, 0)),
                   pl.BlockSpec((nsel, TOKEN_TILE), lambda i: (0, i))],
        out_shape=[jax.ShapeDtypeStruct((n, nsel), jnp.int32), jax.ShapeDtypeStruct((nsel, n), _f32)],
        compiler_params=pltpu.CompilerParams(
            dimension_semantics=("parallel",), vmem_limit_bytes=VMEM_LIMIT),
        name="peer_topk",
    )(qp, keys1, keys2)


def _pack_table(t):
    e, d = t.shape
    bits = lax.bitcast_convert_type(t.astype(_bf16), jnp.uint16).astype(jnp.uint32)
    words = bits[:, :d // 2] | (bits[:, d // 2:] << 16)
    return words.reshape(e * SLAB, LANES)


def _unpack(words):
    lo = pltpu.bitcast(words << 16, _f32)
    hi = pltpu.bitcast(words & jnp.uint32(0xFFFF0000), _f32)
    return lo, hi


def _gather_transposed(idx_ref, t, tab_ref, tile, slice_row):
    row = idx_ref.at[t] if slice_row else None
    for j in range(idx_ref.shape[1]):
        r = pl.multiple_of(row[j] if slice_row else idx_ref[t, j], SLAB)
        tile[pl.ds(j, SLAB, stride=SLAB_STRIDE), :] = tab_ref[pl.ds(r, SLAB), :]


def _tile_chunk(tile, s, nsel):
    return _unpack(tile[s * SLAB_STRIDE:s * SLAB_STRIDE + nsel, :])


def _split_bf16(x):
    hi = x.astype(_bf16)
    return hi, (x - hi.astype(_f32)).astype(_bf16)


def _mxu_right(x, rhs):
    hi, lo = _split_bf16(x)
    return (jnp.dot(hi, rhs, preferred_element_type=_f32) + jnp.dot(lo, rhs, preferred_element_type=_f32))


REDUCE_GROUP = 32
XLU_SHARE = 4
U_TOKENS_PER_STEP = 2


def _peer_u_kernel(idx_ref, xt_ref, g_ref, tab_ref, w_ref, *scratch):
    tiles, parts = scratch[:-1], scratch[-1]
    nsel = idx_ref.shape[1]

    def partial_dots(t, tile):
        x = xt_ref[t]
        acc = jnp.zeros((nsel, LANES), _f32)
        for s in range(SLAB):
            lo, hi = _tile_chunk(tile, s, nsel)
            acc = acc + lo * x[s:s + 1, :] + hi * x[SLAB + s:SLAB + s + 1, :]
        return acc

    n_t = len(tiles)
    last = tiles[-1]
    last[...] = jnp.zeros_like(last)

    def step(i, carry):
        t = n_t * i
        t_prev = jnp.maximum(t - 1, 0)
        parts[t_prev] = partial_dots(t_prev, last)
        for k, tile in enumerate(tiles):
            _gather_transposed(idx_ref, t + k, tab_ref, tile, slice_row=False)
        for k, tile in enumerate(tiles[:-1]):
            parts[t + k] = partial_dots(t + k, tile)
        return carry

    lax.fori_loop(0, TOKEN_TILE // n_t, step, 0)
    parts[TOKEN_TILE - 1] = partial_dots(TOKEN_TILE - 1, last)

    lane = lax.broadcasted_iota(jnp.int32, (nsel, TOKEN_TILE), 1)
    ones = jnp.ones((LANES, LANES), _bf16)

    def reduce_group(gi, act):
        for k in range(REDUCE_GROUP):
            t = gi * REDUCE_GROUP + k
            sums = (jnp.sum(parts[t], axis=1, keepdims=True) if k % XLU_SHARE == XLU_SHARE - 1
                    else _mxu_right(parts[t], ones))
            act = jnp.where(lane == t, sums, act)
        return act

    act = lax.fori_loop(0, TOKEN_TILE // REDUCE_GROUP, reduce_group, jnp.zeros((nsel, TOKEN_TILE), _f32))
    w_ref[...] = g_ref[...] * jax.nn.gelu(act)


def _peer_u(idx, xt, g_t, table):
    n = xt.shape[0]
    d = xt.shape[1] * xt.shape[2]
    nsel = idx.shape[1]
    return pl.pallas_call(
        _peer_u_kernel,
        grid=(n // TOKEN_TILE,),
        in_specs=[pl.BlockSpec((TOKEN_TILE, nsel), lambda i: (i, 0), memory_space=pltpu.SMEM),
                  pl.BlockSpec((TOKEN_TILE, d // LANES, LANES), lambda i: (i, 0, 0)),
                  pl.BlockSpec((nsel, TOKEN_TILE), lambda i: (0, i)),
                  _resident_table_spec(table)],
        out_specs=pl.BlockSpec((nsel, TOKEN_TILE), lambda i: (0, i)),
        out_shape=jax.ShapeDtypeStruct((nsel, n), _f32),
        scratch_shapes=[pltpu.VMEM((SLAB * SLAB_STRIDE, LANES), jnp.uint32)] * U_TOKENS_PER_STEP
                       + [pltpu.VMEM((TOKEN_TILE, nsel, LANES), _f32)],
        compiler_params=pltpu.CompilerParams(
            dimension_semantics=("arbitrary",), vmem_limit_bytes=VMEM_LIMIT),
        name="peer_u_pass",
    )(idx, xt, g_t, table)


def _resident_table_spec(table):
    return pl.BlockSpec(table.shape, lambda i: (0, 0), pipeline_mode=pl.Buffered(1))


def _peer_v_kernel(idx_ref, w_ref, tab_ref, o_ref, tile_a, tile_b, wcol_b):
    nsel = idx_ref.shape[1]
    lane = lax.broadcasted_iota(jnp.int32, (nsel, TOKEN_TILE), 1)

    def weight_column(t):
        w_col = jnp.sum(jnp.where(lane == t, w_ref[...], 0.0), axis=1, keepdims=True)
        return jnp.broadcast_to(w_col, (nsel, LANES))

    def weighted_sum(t, tile, w_col):
        rows_lo, rows_hi = [], []
        for s in range(SLAB):
            lo, hi = _tile_chunk(tile, s, nsel)
            rows_lo.append(jnp.sum(lo * w_col, axis=0, keepdims=True))
            rows_hi.append(jnp.sum(hi * w_col, axis=0, keepdims=True))
        o_ref[t] = jnp.concatenate(rows_lo + rows_hi, axis=0)

    tile_b[...] = jnp.zeros_like(tile_b)
    wcol_b[...] = jnp.zeros_like(wcol_b)

    def two_tokens(i, carry):
        t = 2 * i
        w_a = weight_column(t)
        w_b = weight_column(t + 1)
        weighted_sum(jnp.maximum(t - 1, 0), tile_b, wcol_b[...])
        _gather_transposed(idx_ref, t, tab_ref, tile_a, slice_row=True)
        _gather_transposed(idx_ref, t + 1, tab_ref, tile_b, slice_row=True)
        weighted_sum(t, tile_a, w_a)
        wcol_b[...] = w_b
        return carry

    lax.fori_loop(0, TOKEN_TILE // 2, two_tokens, 0)
    weighted_sum(TOKEN_TILE - 1, tile_b, wcol_b[...])


def _peer_v(idx, w_t, table, d):
    n, nsel = idx.shape
    return pl.pallas_call(
        _peer_v_kernel,
        grid=(n // TOKEN_TILE,),
        in_specs=[pl.BlockSpec((TOKEN_TILE, nsel), lambda i: (i, 0), memory_space=pltpu.SMEM),
                  pl.BlockSpec((nsel, TOKEN_TILE), lambda i: (0, i)),
                  _resident_table_spec(table)],
        out_specs=pl.BlockSpec((TOKEN_TILE, d // LANES, LANES), lambda i: (i, 0, 0)),
        out_shape=jax.ShapeDtypeStruct((n, d // LANES, LANES), _f32),
        scratch_shapes=[pltpu.VMEM((SLAB * SLAB_STRIDE, LANES), jnp.uint32),
                        pltpu.VMEM((SLAB * SLAB_STRIDE, LANES), jnp.uint32),
                        pltpu.VMEM((nsel, LANES), _f32)],
        compiler_params=pltpu.CompilerParams(
            dimension_semantics=("arbitrary",), vmem_limit_bytes=VMEM_LIMIT),
        name="peer_v_pass",
    )(idx, w_t, table)


def _final_kernel(x1_ref, p_ref, g_ref, y_ref):
    y_ref[...] = _rms(x1_ref[...] + p_ref[...].reshape(x1_ref.shape), g_ref[...])


def _final(x1, peer, g):
    n, d = x1.shape
    tm = _row_tile(n)
    row = lambda i: (i, 0)
    return pl.pallas_call(
        _final_kernel,
        grid=(n // tm,),
        in_specs=[pl.BlockSpec((tm, d), row), pl.BlockSpec((tm, d // LANES, LANES), lambda i: (i, 0, 0)),
                  pl.BlockSpec((1, d), lambda i: (0, 0))],
        out_specs=pl.BlockSpec((tm, d), row),
        out_shape=jax.ShapeDtypeStruct((n, d), _f32),
        compiler_params=pltpu.CompilerParams(
            dimension_semantics=("parallel",), vmem_limit_bytes=VMEM_LIMIT),
        name="residual_final_norm",
    )(x1, peer, g)


def _rel_bias(rel_table, n_q, n_k, offset):
    m = jnp.arange(n_k + n_q - 1)
    diag = rel_table[:, jnp.clip(offset + n_q - 1 - m, -MAX_REL, MAX_REL) + MAX_REL].astype(_f32)
    return jnp.stack([diag[:, n_q - 1 - q:n_q - 1 - q + n_k] for q in range(n_q)], axis=1)


def _block_diag_groups(w):
    nb, bi, bo = w.shape
    per = MXU_DIM // bi
    w = w.reshape(nb // per, per, bi, bo)
    eye = jnp.eye(per, dtype=w.dtype)
    return jnp.einsum("gpio,pq->gpiqo", w, eye).reshape(nb // per, per * bi, per * bo)


def _layer(xp, xs, cache_k, cache_v, state_conv, state_lru, norm_mix, w_in, rel_table, conv_w, conv_b,
           lru_wr, lru_br, lru_wi, lru_bi, lru_lambda, w_ba, w_bl, w_o, norm_ffn, peer_wq, keys1, keys2,
           tab_u, tab_v, norm_final):
    bsz, seq, d = xp.shape
    dbsz, dseq, _ = xs.shape
    n_p, n_s = bsz * seq, dbsz * dseq
    xp2, xs2 = xp.reshape(n_p, d), xs.reshape(n_s, d)
    row = lambda v: v.reshape(1, -1)

    w_in_b = w_in.astype(_bf16)
    proj, proj16 = _inproj(xp2, row(norm_mix), w_in_b)
    proj_s, proj16_s = _inproj(xs2, row(norm_mix), w_in_b)

    tiles = seq // BAND
    bias_p = _rel_bias(rel_table, CHUNK, BAND + CHUNK, BAND)
    prev = lambda c: (lambda b, i: (b * tiles + jnp.maximum(i - 1, 0), c))
    cur = lambda c: (lambda b, i: (b * tiles + i, c))
    o_attn_p = _attention(
        proj16, proj, proj, proj, bias_p, out_rows=n_p, n_chunks=BAND // CHUNK, cq=CHUNK, mask_first=True,
        grid=(bsz, tiles), q_map=cur(Q_COL), k_map=cur(K_COL), v_map=cur(V_COL),
        kprev_spec=pl.BlockSpec((BAND, d), prev(K_COL)), vprev_spec=pl.BlockSpec((BAND, d), prev(V_COL)))
    rows_c = cache_k.shape[1]
    bias_s = _rel_bias(rel_table, dseq, rows_c + dseq, rows_c)
    samp = lambda c: (lambda b, i: (b, c))
    cache_spec = pl.BlockSpec((None, rows_c, d), lambda b, i: (b, 0, 0))
    o_attn_s = _attention(
        proj16_s, proj_s, cache_k.reshape(dbsz, rows_c, d), cache_v.reshape(dbsz, rows_c, d), bias_s,
        out_rows=n_s, n_chunks=1, cq=dseq, mask_first=False, grid=(dbsz, 1),
        q_map=samp(Q_COL), k_map=samp(K_COL), v_map=samp(V_COL), kprev_spec=cache_spec, vprev_spec=cache_spec)

    lru_w = (conv_w, row(conv_b), _block_diag_groups(lru_wr).astype(_bf16), row(lru_br),
             _block_diag_groups(lru_wi).astype(_bf16), row(lru_bi), row(lru_lambda))
    pad_state = lambda st: jnp.pad(st, ((0, 0), (SUBLANES - (CONV_W - 1), 0), (0, 0)))
    o_lru_p, h_p = _lru(proj, proj16, jnp.zeros((bsz, SUBLANES, d), _f32), jnp.zeros((bsz, 1, d), _f32), lru_w,
                        n_batch=bsz, seq=seq, tt=256, row0=0)
    o_lru_s, h_s = _lru(proj_s, proj16_s, pad_state(state_conv), state_lru.reshape(dbsz, 1, d), lru_w,
                        n_batch=dbsz, seq=dseq, tt=dseq, row0=0)

    mix_w = (w_ba.astype(_bf16), w_bl.astype(_bf16), w_o.astype(_bf16), row(norm_ffn), peer_wq.astype(_bf16))
    k1, k2 = keys1.astype(_bf16), keys2.astype(_bf16)

    def ffn(x2, o_attn, o_lru, pr):
        x1, xt, qp = _merge(x2, o_attn, o_lru, pr, *mix_w)
        idx, g_t = _topk(qp, k1, k2)
        w_t = _peer_u(idx, xt, g_t, tab_u)
        peer = _peer_v(idx, w_t, tab_v, d)
        return _final(x1, peer, row(norm_final))

    y_p = ffn(xp2, o_attn_p, o_lru_p, proj16).reshape(bsz, seq, d)
    y_s = ffn(xs2, o_attn_s, o_lru_s, proj16_s).reshape(dbsz, dseq, d)

    def window(pr, b, t, rows, c):
        return pr.reshape(b, t, -1)[:, t - rows:, c * d:(c + 1) * d]

    heads = lambda a: a.reshape(a.shape[0], a.shape[1], N_HEADS, HEAD_DIM)
    rows_p = min(BAND, seq)
    keep = CONV_W - 1
    xr_s = jnp.concatenate([state_conv, window(proj_s, dbsz, dseq, dseq, XR_COL)], axis=1)
    return (y_p, y_s, heads(window(proj, bsz, seq, rows_p, K_COL)), heads(window(proj, bsz, seq, rows_p, V_COL)),
            window(proj, bsz, seq, keep, XR_COL), h_p.reshape(bsz, d),
            heads(window(proj_s, dbsz, dseq, dseq, K_COL)), heads(window(proj_s, dbsz, dseq, dseq, V_COL)),
            xr_s[:, xr_s.shape[1] - keep:], h_s.reshape(dbsz, d))


def kernel(x_prompt, x_sample, cache_k, cache_v, state_conv, state_lru, norm_mix, w_in, rel_table, conv_w, conv_b, lru_wr, lru_br, lru_wi, lru_bi, lru_lambda, w_branch_attn, w_branch_lru, w_out, norm_ffn, peer_wq, peer_keys1, peer_keys2, peer_u, peer_v, norm_final):
    depth = w_in.shape[0]
    assert depth == 1, "the final norm is fused into the single layer's last kernel"
    l = 0
    (y_prompt, y_sample, k_p, v_p, c_p, h_p, k_s, v_s, c_s, h_s) = _layer(
        x_prompt, x_sample, cache_k[l], cache_v[l], state_conv[l], state_lru[l], norm_mix[l], w_in[l],
        rel_table[l], conv_w[l], conv_b[l], lru_wr[l], lru_br[l], lru_wi[l], lru_bi[l], lru_lambda[l],
        w_branch_attn[l], w_branch_lru[l], w_out[l], norm_ffn[l], peer_wq[l], peer_keys1[l], peer_keys2[l],
        _pack_table(peer_u[l]), _pack_table(peer_v[l]), norm_final)
    stack = lambda v: v[None]
    return (y_prompt, y_sample, stack(k_p), stack(v_p), stack(c_p), stack(h_p),
            stack(k_s), stack(v_s), stack(c_s), stack(h_s))
```

```python
import functools

import jax
import jax.numpy as jnp
from jax import lax
from jax.experimental import pallas as pl
from jax.experimental.pallas import tpu as pltpu

CHUNK = 64
LEFT_CHUNKS = 8
BAND = LEFT_CHUNKS * CHUNK
N_HEADS = 16
HEAD_DIM = 64
ATTN_SCALE = HEAD_DIM ** -0.5
MAX_REL = 128
CONV_W = 4
LRU_C = 8.0
PEER_HEADS = 8
N_KEYS = 128
TOPK = 16
EPS = 1e-6
NEG_INF = -1e30

LANES = 128
SUBLANES = 8
MXU_DIM = 256
TOKEN_TILE = 128
SLAB = 4
SLAB_STRIDE = 136
VMEM_LIMIT = 56 * 1024 * 1024

_bf16 = jnp.bfloat16
_f32 = jnp.float32


def _row_tile(n_rows):
    for t in (512, 384, 256, 128):
        if n_rows % t == 0:
            return t
    raise ValueError(f"token count {n_rows} must be a multiple of 128")


def _rms(x, g):
    return x * lax.rsqrt(jnp.mean(x * x, axis=-1, keepdims=True) + EPS) * g


F32_BLOCKS = (1, 2, 3)
BF16_BLOCKS = (0, 4, 5, 6)
K_COL, V_COL, XR_COL = 0, 1, 2
Q_COL, GB_COL, GA_COL, GR_COL = 0, 1, 2, 3


def _inproj_kernel(x_ref, g_ref, w_ref, o32_ref, o16_ref):
    d = x_ref.shape[1]
    xn = _rms(x_ref[...], g_ref[...]).astype(_bf16)
    for j in range(w_ref.shape[1] // d):
        y = jnp.dot(xn, w_ref[:, j * d:(j + 1) * d], preferred_element_type=_f32)
        if j in F32_BLOCKS:
            c = F32_BLOCKS.index(j)
            o32_ref[:, c * d:(c + 1) * d] = y
        else:
            c = BF16_BLOCKS.index(j)
            o16_ref[:, c * d:(c + 1) * d] = y.astype(_bf16)


def _inproj(x, g, w):
    n, d = x.shape
    cols = w.shape[1]
    assert cols == (len(F32_BLOCKS) + len(BF16_BLOCKS)) * d
    tm = 256 if n % 256 == 0 else TOKEN_TILE
    n32, n16 = len(F32_BLOCKS) * d, len(BF16_BLOCKS) * d
    return pl.pallas_call(
        _inproj_kernel,
        grid=(n // tm,),
        in_specs=[pl.BlockSpec((tm, d), lambda i: (i, 0)),
                  pl.BlockSpec((1, d), lambda i: (0, 0)),
                  pl.BlockSpec((d, cols), lambda i: (0, 0), pipeline_mode=pl.Buffered(1))],
        out_specs=[pl.BlockSpec((tm, n32), lambda i: (i, 0)), pl.BlockSpec((tm, n16), lambda i: (i, 0))],
        out_shape=[jax.ShapeDtypeStruct((n, n32), _f32), jax.ShapeDtypeStruct((n, n16), _bf16)],
        compiler_params=pltpu.CompilerParams(
            dimension_semantics=("parallel",), vmem_limit_bytes=VMEM_LIMIT),
        name="inproj",
    )(x, g, w)


def _attn_kernel(q_ref, kp_ref, kc_ref, vp_ref, vc_ref, bias_ref, o_ref, kcat, vcat,
                 *, n_chunks, cq, mask_first):
    nq = n_chunks * cq
    span = BAND + cq
    kcat[0:BAND, :] = kp_ref[...].astype(_bf16)
    kcat[BAND:BAND + nq, :] = kc_ref[...].astype(_bf16)
    vcat[0:BAND, :] = vp_ref[...].astype(_bf16)
    vcat[BAND:BAND + nq, :] = vc_ref[...].astype(_bf16)
    first = pl.program_id(1) == 0
    pair = 2 * HEAD_DIM
    low_head = lax.broadcasted_iota(jnp.int32, (cq, pair), 1) < HEAD_DIM

    def chunk(cl, carry, masked):
        r0 = pl.multiple_of(cl * cq, cq)
        q = q_ref[pl.ds(r0, cq), :] * ATTN_SCALE
        kw = kcat[pl.ds(r0, span), :]
        vw = vcat[pl.ds(r0, span), :]
        if masked:
            valid = lax.broadcasted_iota(jnp.int32, (cq, span), 1) + r0 >= BAND
        scores = []
        for pr in range(N_HEADS // 2):
            sl = slice(pr * pair, (pr + 1) * pair)
            q_pair = jnp.concatenate([jnp.where(low_head, q[:, sl], 0.0), jnp.where(low_head, 0.0, q[:, sl])],
                                     axis=0).astype(_bf16)
            s_pair = lax.dot_general(q_pair, kw[:, sl], (((1,), (1,)), ((), ())), preferred_element_type=_f32)
            scores += [s_pair[:cq], s_pair[cq:]]
        probs, inv = [], []
        for h in range(N_HEADS):
            s = scores[h] + bias_ref[h]
            if masked:
                s = jnp.where(valid, s, NEG_INF)
            p = jnp.exp(s - jnp.max(s, axis=-1, keepdims=True))
            inv.append(1.0 / jnp.sum(p, axis=-1, keepdims=True))
            probs.append(p.astype(_bf16))
        outs = []
        for pr in range(N_HEADS // 2):
            vp = vw[:, pr * pair:(pr + 1) * pair]
            o_pair = jnp.dot(jnp.concatenate([probs[2 * pr], probs[2 * pr + 1]], axis=0), vp,
                             preferred_element_type=_f32)
            outs.append(jnp.where(low_head, o_pair[:cq] * inv[2 * pr], o_pair[cq:] * inv[2 * pr + 1]))
        o_ref[pl.ds(r0, cq), :] = jnp.concatenate(outs, axis=-1).astype(o_ref.dtype)
        return carry

    def run(masked):
        lax.fori_loop(0, n_chunks, functools.partial(chunk, masked=masked), 0)

    if mask_first:
        pl.when(first)(functools.partial(run, True))
        pl.when(jnp.logical_not(first))(functools.partial(run, False))
    else:
        run(False)


def _attention(q_src, kv_src, kprev_src, vprev_src, bias, *, out_rows, n_chunks, cq, mask_first, grid,
               q_map, k_map, v_map, kprev_spec, vprev_spec):
    nq = n_chunks * cq
    d = N_HEADS * HEAD_DIM
    span = BAND + cq
    kern = functools.partial(_attn_kernel, n_chunks=n_chunks, cq=cq, mask_first=mask_first)
    return pl.pallas_call(
        kern,
        grid=grid,
        in_specs=[pl.BlockSpec((nq, d), q_map),
                  kprev_spec,
                  pl.BlockSpec((nq, d), k_map),
                  vprev_spec,
                  pl.BlockSpec((nq, d), v_map),
                  pl.BlockSpec((N_HEADS, cq, span), lambda b, i: (0, 0, 0))],
        out_specs=pl.BlockSpec((nq, d), lambda b, i, g=grid: (b * g[1] + i, 0)),
        out_shape=jax.ShapeDtypeStruct((out_rows, d), _bf16),
        scratch_shapes=[pltpu.VMEM((BAND + nq, d), _bf16), pltpu.VMEM((BAND + nq, d), _bf16)],
        compiler_params=pltpu.CompilerParams(
            dimension_semantics=("parallel", "arbitrary"), vmem_limit_bytes=VMEM_LIMIT),
        name="band_attention",
    )(q_src, kprev_src, kv_src, vprev_src, kv_src, bias)


def _shift_rows(x, s, fill, row):
    return jnp.where(row >= s, pltpu.roll(x, s, axis=0), fill)


def _lru_kernel(x_ref, gb_ref, cinit_ref, h0_ref, cw_ref, cb_ref, wr_ref, br_ref, wi_ref, bi_ref,
                lam_ref, o_ref, hl_ref, xbuf, hcar, *, tt):
    d = x_ref.shape[-1]

    @pl.when(pl.program_id(1) == 0)
    def _():
        xbuf[0:SUBLANES, :] = cinit_ref[...]
        hcar[...] = h0_ref[...]

    xbuf[SUBLANES:SUBLANES + tt, :] = x_ref[...]
    base = SUBLANES - (CONV_W - 1)
    xc = cb_ref[...]
    for k in range(CONV_W):
        xc = xc + xbuf[pl.ds(base + k, tt), :] * cw_ref[k:k + 1, :]
    xbuf[0:SUBLANES, :] = xbuf[tt:tt + SUBLANES, :]

    xcb = xc.astype(_bf16)
    n_groups = d // MXU_DIM
    r_parts, i_parts = [], []
    for g in range(n_groups):
        xs = xcb[:, g * MXU_DIM:(g + 1) * MXU_DIM]
        r_parts.append(jnp.dot(xs, wr_ref[g], preferred_element_type=_f32))
        i_parts.append(jnp.dot(xs, wi_ref[g], preferred_element_type=_f32))
    r = jax.nn.sigmoid(jnp.concatenate(r_parts, axis=-1) + br_ref[...])
    gi = jax.nn.sigmoid(jnp.concatenate(i_parts, axis=-1) + bi_ref[...])
    z = -lam_ref[...]
    softplus = jnp.maximum(z, 0.0) + jnp.log(1.0 + jnp.exp(-jnp.abs(z)))
    log_a = (-LRU_C) * r * softplus
    a = jnp.exp(log_a)
    b = jnp.sqrt(1.0 - a * a) * (gi * xc)

    row = lax.broadcasted_iota(jnp.int32, (tt, d), 0) % SUBLANES
    s = 1
    while s < SUBLANES:
        b = a * _shift_rows(b, s, 0.0, row) + b
        a = a * _shift_rows(a, s, 1.0, row)
        s *= 2
    h_in = hcar[...]
    groups = []
    for g in range(tt // SUBLANES):
        rows = slice(g * SUBLANES, (g + 1) * SUBLANES)
        h_g = a[rows] * h_in + b[rows]
        groups.append(h_g)
        h_in = h_g[SUBLANES - 1:SUBLANES, :]
    h = jnp.concatenate(groups, axis=0)
    h_last = h[tt - 1:tt, :]
    hcar[...] = h_last
    hl_ref[...] = h_last
    o_ref[...] = (h * jax.nn.gelu(gb_ref[...].astype(_f32))).astype(o_ref.dtype)


def _lru(proj32, proj16, conv_init, h0, weights, *, n_batch, seq, tt, row0):
    d = conv_init.shape[-1]
    steps = seq // tt
    blk0 = row0 // tt
    (cw, cb, wr, br, wi, bi, lam) = weights
    const2 = lambda b, i: (0, 0)
    const3 = lambda b, i: (0, 0, 0)
    kern = functools.partial(_lru_kernel, tt=tt)
    return pl.pallas_call(
        kern,
        grid=(n_batch, steps),
        in_specs=[pl.BlockSpec((tt, d), lambda b, i: (blk0 + b * steps + i, XR_COL)),
                  pl.BlockSpec((tt, d), lambda b, i: (blk0 + b * steps + i, GB_COL)),
                  pl.BlockSpec((None, SUBLANES, d), lambda b, i: (b, 0, 0)),
                  pl.BlockSpec((None, 1, d), lambda b, i: (b, 0, 0)),
                  pl.BlockSpec((CONV_W, d), const2),
                  pl.BlockSpec((1, d), const2),
                  pl.BlockSpec(wr.shape, const3),
                  pl.BlockSpec((1, d), const2),
                  pl.BlockSpec(wi.shape, const3),
                  pl.BlockSpec((1, d), const2),
                  pl.BlockSpec((1, d), const2)],
        out_specs=[pl.BlockSpec((tt, d), lambda b, i: (b * steps + i, 0)),
                   pl.BlockSpec((None, 1, d), lambda b, i: (b, 0, 0))],
        out_shape=[jax.ShapeDtypeStruct((n_batch * seq, d), _bf16),
                   jax.ShapeDtypeStruct((n_batch, 1, d), _f32)],
        scratch_shapes=[pltpu.VMEM((tt + 2 * SUBLANES, d), _f32), pltpu.VMEM((1, d), _f32)],
        compiler_params=pltpu.CompilerParams(
            dimension_semantics=("parallel", "arbitrary"), vmem_limit_bytes=VMEM_LIMIT),
        name="conv_rglru",
    )(proj32, proj16, conv_init, h0, cw, cb, wr, br, wi, bi, lam)


def _merge_kernel(x_ref, oa_ref, ol_ref, ga_ref, gr_ref, wba_ref, wbl_ref, wo_ref, gn_ref, wq_ref,
                  x1_ref, xt_ref, qp_ref):
    ya = jnp.dot(oa_ref[...], wba_ref[...], preferred_element_type=_f32)
    yl = jnp.dot(ol_ref[...], wbl_ref[...], preferred_element_type=_f32)
    merged = (jax.nn.sigmoid(ga_ref[...].astype(_f32)) * ya
              + jax.nn.sigmoid(gr_ref[...].astype(_f32)) * yl)
    x1 = x_ref[...] + jnp.dot(merged.astype(_bf16), wo_ref[...], preferred_element_type=_f32)
    x1_ref[...] = x1
    xt = _rms(x1, gn_ref[...])
    xt_ref[...] = xt.reshape(xt_ref.shape)
    qp_ref[...] = jnp.dot(xt.astype(_bf16), wq_ref[...], preferred_element_type=_f32).astype(_bf16)


def _merge(x, o_attn, o_lru, proj16, w_ba, w_bl, w_o, g_ffn, wq):
    n, d = x.shape
    tm = _row_tile(n)
    nq = wq.shape[1]
    row = lambda i: (i, 0)
    const = lambda i: (0, 0)
    return pl.pallas_call(
        _merge_kernel,
        grid=(n // tm,),
        in_specs=[pl.BlockSpec((tm, d), row),
                  pl.BlockSpec((tm, d), row),
                  pl.BlockSpec((tm, d), row),
                  pl.BlockSpec((tm, d), lambda i: (i, GA_COL)),
                  pl.BlockSpec((tm, d), lambda i: (i, GR_COL)),
                  pl.BlockSpec((d, d), const, pipeline_mode=pl.Buffered(1)),
                  pl.BlockSpec((d, d), const, pipeline_mode=pl.Buffered(1)),
                  pl.BlockSpec((d, d), const, pipeline_mode=pl.Buffered(1)),
                  pl.BlockSpec((1, d), const),
                  pl.BlockSpec((d, nq), const, pipeline_mode=pl.Buffered(1))],
        out_specs=[pl.BlockSpec((tm, d), row), pl.BlockSpec((tm, d // LANES, LANES), lambda i: (i, 0, 0)),
                   pl.BlockSpec((tm, nq), row)],
        out_shape=[jax.ShapeDtypeStruct((n, d), _f32), jax.ShapeDtypeStruct((n, d // LANES, LANES), _f32),
                   jax.ShapeDtypeStruct((n, nq), _bf16)],
        compiler_params=pltpu.CompilerParams(
            dimension_semantics=("parallel",), vmem_limit_bytes=VMEM_LIMIT),
        name="merge_outproj_peerq",
    )(x, o_attn, o_lru, proj16, proj16, w_ba, w_bl, w_o, g_ffn, wq)


def _top16(s, payload=None):
    n_rows = s.shape[0]
    row = lax.broadcasted_iota(jnp.int32, s.shape, 0).astype(_f32)
    groups = [slice(g, g + SUBLANES) for g in range(0, n_rows, SUBLANES)]
    vals, picks = [], []
    for _ in range(TOPK):
        level = [(s[g], row[g], None if payload is None else payload[g]) for g in groups]
        while len(level) > 1:
            nxt = []
            for k in range(0, len(level) - 1, 2):
                (va, ia, pa), (vb, ib, pb) = level[k], level[k + 1]
                left = va >= vb
                nxt.append((jnp.maximum(va, vb), jnp.where(left, ia, ib),
                            None if pa is None else jnp.where(left, pa, pb)))
            if len(level) % 2:
                nxt.append(level[-1])
            level = nxt
        v8, i8, p8 = level[0]
        m = jnp.max(v8, axis=0, keepdims=True)
        am = jnp.min(jnp.where(v8 == m, i8, float(n_rows)), axis=0, keepdims=True)
        vals.append(m)
        if payload is None:
            picks.append(am)
        else:
            picks.append(jnp.max(jnp.where(i8 == am, p8, -1.0), axis=0, keepdims=True))
        s = jnp.where(row == am, -jnp.inf, s)
    return jnp.concatenate(vals, axis=0), jnp.concatenate(picks, axis=0)


def _candidates(a, b, mask_out):
    j8 = lax.broadcasted_iota(jnp.int32, (SUBLANES, a.shape[1]), 0)
    pieces = [a[0:1, :] + b]
    for i in range(1, SUBLANES):
        piece = a[i:i + 1, :] + b[0:SUBLANES, :]
        if mask_out:
            piece = jnp.where(j8 < TOPK // (i + 1), piece, -jnp.inf)
        pieces.append(piece)
    pieces.append(a[SUBLANES:TOPK, :] + b[0:1, :])
    return jnp.concatenate(pieces, axis=0)


def _topk_kernel(qp_ref, k1_ref, k2_ref, idx_ref, g_ref):
    nt = (((1,), (1,)), ((), ()))
    picks = []
    for h in range(PEER_HEADS):
        c0 = h * 2 * N_KEYS
        q1 = qp_ref[:, c0:c0 + N_KEYS]
        q2 = qp_ref[:, c0 + N_KEYS:c0 + 2 * N_KEYS]
        s1 = lax.dot_general(k1_ref[h], q1, nt, preferred_element_type=_f32)
        s2 = lax.dot_general(k2_ref[h], q2, nt, preferred_element_type=_f32)
        v1, i1 = _top16(s1)
        v2, i2 = _top16(s2)
        cand_s = _candidates(v1, v2, True)
        cand_i = _candidates(i1 * float(N_KEYS), i2, False)
        top_s, e_idx = _top16(cand_s, cand_i)
        p = jnp.exp(top_s - jnp.max(top_s, axis=0, keepdims=True))
        g = p / jnp.sum(p, axis=0, keepdims=True)
        picks.append(e_idx)
        g_ref[h * TOPK:(h + 1) * TOPK, :] = g
    idx_ref[...] = jnp.concatenate(picks, axis=0).T.astype(jnp.int32) * SLAB


def _topk(qp, keys1, keys2):
    n = qp.shape[0]
    nsel = PEER_HEADS * TOPK
    const3 = lambda i: (0, 0, 0)
    return pl.pallas_call(
        _topk_kernel,
        grid=(n // TOKEN_TILE,),
        in_specs=[pl.BlockSpec((TOKEN_TILE, qp.shape[1]), lambda i: (i, 0)),
                  pl.BlockSpec(keys1.shape, const3),
                  pl.BlockSpec(keys2.shape, const3)],
        out_specs=[pl.BlockSpec((TOKEN_TILE, nsel), lambda i: (i, 0)),
                   pl.BlockSpec((nsel, TOKEN_TILE), lambda i: (0, i))],
        out_shape=[jax.ShapeDtypeStruct((n, nsel), jnp.int32), jax.ShapeDtypeStruct((nsel, n), _f32)],
        compiler_params=pltpu.CompilerParams(
            dimension_semantics=("parallel",), vmem_limit_bytes=VMEM_LIMIT),
        name="peer_topk",
    )(qp, keys1, keys2)


def _pack_table(t):
    e, d = t.shape
    bits = lax.bitcast_convert_type(t.astype(_bf16), jnp.uint16).astype(jnp.uint32)
    words = bits[:, :d // 2] | (bits[:, d // 2:] << 16)
    return words.reshape(e * SLAB, LANES)


def _unpack(words):
    lo = pltpu.bitcast(words << 16, _f32)
    hi = pltpu.bitcast(words & jnp.uint32(0xFFFF0000), _f32)
    return lo, hi


def _gather_transposed(idx_ref, t, tab_ref, tile, slice_row):
    row = idx_ref.at[t] if slice_row else None
    for j in range(idx_ref.shape[1]):
        r = pl.multiple_of(row[j] if slice_row else idx_ref[t, j], SLAB)
        tile[pl.ds(j, SLAB, stride=SLAB_STRIDE), :] = tab_ref[pl.ds(r, SLAB), :]


def _tile_chunk(tile, s, nsel):
    return _unpack(tile[s * SLAB_STRIDE:s * SLAB_STRIDE + nsel, :])


def _split_bf16(x):
    hi = x.astype(_bf16)
    return hi, (x - hi.astype(_f32)).astype(_bf16)


def _mxu_right(x, rhs):
    hi, lo = _split_bf16(x)
    return (jnp.dot(hi, rhs, preferred_element_type=_f32) + jnp.dot(lo, rhs, preferred_element_type=_f32))


REDUCE_GROUP = 32
XLU_SHARE = 2
U_TOKENS_PER_STEP = 2


def _peer_u_kernel(idx_ref, xt_ref, g_ref, tab_ref, w_ref, *scratch):
    tiles, parts = scratch[:-1], scratch[-1]
    nsel = idx_ref.shape[1]

    def partial_dots(t, tile):
        x = xt_ref[t]
        acc = jnp.zeros((nsel, LANES), _f32)
        for s in range(SLAB):
            lo, hi = _tile_chunk(tile, s, nsel)
            acc = acc + lo * x[s:s + 1, :] + hi * x[SLAB + s:SLAB + s + 1, :]
        return acc

    n_t = len(tiles)
    last = tiles[-1]
    last[...] = jnp.zeros_like(last)

    def step(i, carry):
        t = n_t * i
        t_prev = jnp.maximum(t - 1, 0)
        parts[t_prev] = partial_dots(t_prev, last)
        for k, tile in enumerate(tiles):
            _gather_transposed(idx_ref, t + k, tab_ref, tile, slice_row=False)
        for k, tile in enumerate(tiles[:-1]):
            parts[t + k] = partial_dots(t + k, tile)
        return carry

    lax.fori_loop(0, TOKEN_TILE // n_t, step, 0)
    parts[TOKEN_TILE - 1] = partial_dots(TOKEN_TILE - 1, last)

    lane = lax.broadcasted_iota(jnp.int32, (nsel, TOKEN_TILE), 1)
    ones = jnp.ones((LANES, LANES), _bf16)

    def reduce_group(gi, act):
        for k in range(REDUCE_GROUP):
            t = gi * REDUCE_GROUP + k
            sums = (jnp.sum(parts[t], axis=1, keepdims=True) if k % XLU_SHARE == XLU_SHARE - 1
                    else _mxu_right(parts[t], ones))
            act = jnp.where(lane == t, sums, act)
        return act

    act = lax.fori_loop(0, TOKEN_TILE // REDUCE_GROUP, reduce_group, jnp.zeros((nsel, TOKEN_TILE), _f32))
    w_ref[...] = g_ref[...] * jax.nn.gelu(act)


def _peer_u(idx, xt, g_t, table):
    n = xt.shape[0]
    d = xt.shape[1] * xt.shape[2]
    nsel = idx.shape[1]
    return pl.pallas_call(
        _peer_u_kernel,
        grid=(n // TOKEN_TILE,),
        in_specs=[pl.BlockSpec((TOKEN_TILE, nsel), lambda i: (i, 0), memory_space=pltpu.SMEM),
                  pl.BlockSpec((TOKEN_TILE, d // LANES, LANES), lambda i: (i, 0, 0)),
                  pl.BlockSpec((nsel, TOKEN_TILE), lambda i: (0, i)),
                  _resident_table_spec(table)],
        out_specs=pl.BlockSpec((nsel, TOKEN_TILE), lambda i: (0, i)),
        out_shape=jax.ShapeDtypeStruct((nsel, n), _f32),
        scratch_shapes=[pltpu.VMEM((SLAB * SLAB_STRIDE, LANES), jnp.uint32)] * U_TOKENS_PER_STEP
                       + [pltpu.VMEM((TOKEN_TILE, nsel, LANES), _f32)],
        compiler_params=pltpu.CompilerParams(
            dimension_semantics=("arbitrary",), vmem_limit_bytes=VMEM_LIMIT),
        name="peer_u_pass",
    )(idx, xt, g_t, table)


def _resident_table_spec(table):
    return pl.BlockSpec(table.shape, lambda i: (0, 0), pipeline_mode=pl.Buffered(1))


def _peer_v_kernel(idx_ref, w_ref, tab_ref, o_ref, tile_a, tile_b, wcol_b):
    nsel = idx_ref.shape[1]
    lane = lax.broadcasted_iota(jnp.int32, (nsel, TOKEN_TILE), 1)

    def weight_column(t):
        w_col = jnp.sum(jnp.where(lane == t, w_ref[...], 0.0), axis=1, keepdims=True)
        return jnp.broadcast_to(w_col, (nsel, LANES))

    def weighted_sum(t, tile, w_col):
        rows_lo, rows_hi = [], []
        for s in range(SLAB):
            lo, hi = _tile_chunk(tile, s, nsel)
            rows_lo.append(jnp.sum(lo * w_col, axis=0, keepdims=True))
            rows_hi.append(jnp.sum(hi * w_col, axis=0, keepdims=True))
        o_ref[t] = jnp.concatenate(rows_lo + rows_hi, axis=0)

    tile_b[...] = jnp.zeros_like(tile_b)
    wcol_b[...] = jnp.zeros_like(wcol_b)

    def two_tokens(i, carry):
        t = 2 * i
        w_a = weight_column(t)
        w_b = weight_column(t + 1)
        weighted_sum(jnp.maximum(t - 1, 0), tile_b, wcol_b[...])
        _gather_transposed(idx_ref, t, tab_ref, tile_a, slice_row=True)
        _gather_transposed(idx_ref, t + 1, tab_ref, tile_b, slice_row=True)
        weighted_sum(t, tile_a, w_a)
        wcol_b[...] = w_b
        return carry

    lax.fori_loop(0, TOKEN_TILE // 2, two_tokens, 0)
    weighted_sum(TOKEN_TILE - 1, tile_b, wcol_b[...])


def _peer_v(idx, w_t, table, d):
    n, nsel = idx.shape
    return pl.pallas_call(
        _peer_v_kernel,
        grid=(n // TOKEN_TILE,),
        in_specs=[pl.BlockSpec((TOKEN_TILE, nsel), lambda i: (i, 0), memory_space=pltpu.SMEM),
                  pl.BlockSpec((nsel, TOKEN_TILE), lambda i: (0, i)),
                  _resident_table_spec(table)],
        out_specs=pl.BlockSpec((TOKEN_TILE, d // LANES, LANES), lambda i: (i, 0, 0)),
        out_shape=jax.ShapeDtypeStruct((n, d // LANES, LANES), _f32),
        scratch_shapes=[pltpu.VMEM((SLAB * SLAB_STRIDE, LANES), jnp.uint32),
                        pltpu.VMEM((SLAB * SLAB_STRIDE, LANES), jnp.uint32),
                        pltpu.VMEM((nsel, LANES), _f32)],
        compiler_params=pltpu.CompilerParams(
            dimension_semantics=("arbitrary",), vmem_limit_bytes=VMEM_LIMIT),
        name="peer_v_pass",
    )(idx, w_t, table)


def _final_kernel(x1_ref, p_ref, g_ref, y_ref):
    y_ref[...] = _rms(x1_ref[...] + p_ref[...].reshape(x1_ref.shape), g_ref[...])


def _final(x1, peer, g):
    n, d = x1.shape
    tm = _row_tile(n)
    row = lambda i: (i, 0)
    return pl.pallas_call(
        _final_kernel,
        grid=(n // tm,),
        in_specs=[pl.BlockSpec((tm, d), row), pl.BlockSpec((tm, d // LANES, LANES), lambda i: (i, 0, 0)),
                  pl.BlockSpec((1, d), lambda i: (0, 0))],
        out_specs=pl.BlockSpec((tm, d), row),
        out_shape=jax.ShapeDtypeStruct((n, d), _f32),
        compiler_params=pltpu.CompilerParams(
            dimension_semantics=("parallel",), vmem_limit_bytes=VMEM_LIMIT),
        name="residual_final_norm",
    )(x1, peer, g)


def _rel_bias(rel_table, n_q, n_k, offset):
    m = jnp.arange(n_k + n_q - 1)
    diag = rel_table[:, jnp.clip(offset + n_q - 1 - m, -MAX_REL, MAX_REL) + MAX_REL].astype(_f32)
    return jnp.stack([diag[:, n_q - 1 - q:n_q - 1 - q + n_k] for q in range(n_q)], axis=1)


def _block_diag_groups(w):
    nb, bi, bo = w.shape
    per = MXU_DIM // bi
    w = w.reshape(nb // per, per, bi, bo)
    eye = jnp.eye(per, dtype=w.dtype)
    return jnp.einsum("gpio,pq->gpiqo", w, eye).reshape(nb // per, per * bi, per * bo)


def _layer(xp, xs, cache_k, cache_v, state_conv, state_lru, norm_mix, w_in, rel_table, conv_w, conv_b,
           lru_wr, lru_br, lru_wi, lru_bi, lru_lambda, w_ba, w_bl, w_o, norm_ffn, peer_wq, keys1, keys2,
           tab_u, tab_v, norm_final):
    bsz, seq, d = xp.shape
    dbsz, dseq, _ = xs.shape
    n_p, n_s = bsz * seq, dbsz * dseq
    xp2, xs2 = xp.reshape(n_p, d), xs.reshape(n_s, d)
    row = lambda v: v.reshape(1, -1)

    w_in_b = w_in.astype(_bf16)
    proj, proj16 = _inproj(xp2, row(norm_mix), w_in_b)
    proj_s, proj16_s = _inproj(xs2, row(norm_mix), w_in_b)

    tiles = seq // BAND
    bias_p = _rel_bias(rel_table, CHUNK, BAND + CHUNK, BAND)
    prev = lambda c: (lambda b, i: (b * tiles + jnp.maximum(i - 1, 0), c))
    cur = lambda c: (lambda b, i: (b * tiles + i, c))
    o_attn_p = _attention(
        proj16, proj, proj, proj, bias_p, out_rows=n_p, n_chunks=BAND // CHUNK, cq=CHUNK, mask_first=True,
        grid=(bsz, tiles), q_map=cur(Q_COL), k_map=cur(K_COL), v_map=cur(V_COL),
        kprev_spec=pl.BlockSpec((BAND, d), prev(K_COL)), vprev_spec=pl.BlockSpec((BAND, d), prev(V_COL)))
    rows_c = cache_k.shape[1]
    bias_s = _rel_bias(rel_table, dseq, rows_c + dseq, rows_c)
    samp = lambda c: (lambda b, i: (b, c))
    cache_spec = pl.BlockSpec((None, rows_c, d), lambda b, i: (b, 0, 0))
    o_attn_s = _attention(
        proj16_s, proj_s, cache_k.reshape(dbsz, rows_c, d), cache_v.reshape(dbsz, rows_c, d), bias_s,
        out_rows=n_s, n_chunks=1, cq=dseq, mask_first=False, grid=(dbsz, 1),
        q_map=samp(Q_COL), k_map=samp(K_COL), v_map=samp(V_COL), kprev_spec=cache_spec, vprev_spec=cache_spec)

    lru_w = (conv_w, row(conv_b), _block_diag_groups(lru_wr).astype(_bf16), row(lru_br),
             _block_diag_groups(lru_wi).astype(_bf16), row(lru_bi), row(lru_lambda))
    pad_state = lambda st: jnp.pad(st, ((0, 0), (SUBLANES - (CONV_W - 1), 0), (0, 0)))
    o_lru_p, h_p = _lru(proj, proj16, jnp.zeros((bsz, SUBLANES, d), _f32), jnp.zeros((bsz, 1, d), _f32), lru_w,
                        n_batch=bsz, seq=seq, tt=256, row0=0)
    o_lru_s, h_s = _lru(proj_s, proj16_s, pad_state(state_conv), state_lru.reshape(dbsz, 1, d), lru_w,
                        n_batch=dbsz, seq=dseq, tt=dseq, row0=0)

    mix_w = (w_ba.astype(_bf16), w_bl.astype(_bf16), w_o.astype(_bf16), row(norm_ffn), peer_wq.astype(_bf16))
    k1, k2 = keys1.astype(_bf16), keys2.astype(_bf16)

    def ffn(x2, o_attn, o_lru, pr):
        x1, xt, qp = _merge(x2, o_attn, o_lru, pr, *mix_w)
        idx, g_t = _topk(qp, k1, k2)
        w_t = _peer_u(idx, xt, g_t, tab_u)
        peer = _peer_v(idx, w_t, tab_v, d)
        return _final(x1, peer, row(norm_final))

    y_p = ffn(xp2, o_attn_p, o_lru_p, proj16).reshape(bsz, seq, d)
    y_s = ffn(xs2, o_attn_s, o_lru_s, proj16_s).reshape(dbsz, dseq, d)

    def window(pr, b, t, rows, c):
        return pr.reshape(b, t, -1)[:, t - rows:, c * d:(c + 1) * d]

    heads = lambda a: a.reshape(a.shape[0], a.shape[1], N_HEADS, HEAD_DIM)
    rows_p = min(BAND, seq)
    keep = CONV_W - 1
    xr_s = jnp.concatenate([state_conv, window(proj_s, dbsz, dseq, dseq, XR_COL)], axis=1)
    return (y_p, y_s, heads(window(proj, bsz, seq, rows_p, K_COL)), heads(window(proj, bsz, seq, rows_p, V_COL)),
            window(proj, bsz, seq, keep, XR_COL), h_p.reshape(bsz, d),
            heads(window(proj_s, dbsz, dseq, dseq, K_COL)), heads(window(proj_s, dbsz, dseq, dseq, V_COL)),
            xr_s[:, xr_s.shape[1] - keep:], h_s.reshape(dbsz, d))


def kernel(x_prompt, x_sample, cache_k, cache_v, state_conv, state_lru, norm_mix, w_in, rel_table, conv_w, conv_b, lru_wr, lru_br, lru_wi, lru_bi, lru_lambda, w_branch_attn, w_branch_lru, w_out, norm_ffn, peer_wq, peer_keys1, peer_keys2, peer_u, peer_v, norm_final):
    depth = w_in.shape[0]
    assert depth == 1, "the final norm is fused into the single layer's last kernel"
    l = 0
    (y_prompt, y_sample, k_p, v_p, c_p, h_p, k_s, v_s, c_s, h_s) = _layer(
        x_prompt, x_sample, cache_k[l], cache_v[l], state_conv[l], state_lru[l], norm_mix[l], w_in[l],
        rel_table[l], conv_w[l], conv_b[l], lru_wr[l], lru_br[l], lru_wi[l], lru_bi[l], lru_lambda[l],
        w_branch_attn[l], w_branch_lru[l], w_out[l], norm_ffn[l], peer_wq[l], peer_keys1[l], peer_keys2[l],
        _pack_table(peer_u[l]), _pack_table(peer_v[l]), norm_final)
    stack = lambda v: v[None]
    return (y_prompt, y_sample, stack(k_p), stack(v_p), stack(c_p), stack(h_p),
            stack(k_s), stack(v_s), stack(c_s), stack(h_s))
```
